```python
import math
import jax, jax.numpy as jnp
from jax import lax
import numpy as np

D_MODEL = 1024
BATCH = 8
SEQ = 4096
DEPTH = 1

CHUNK = 64
Q_BLOCK = 128
D_MIX = D_MODEL
D_ATT = D_MIX // 2
D_RNN = D_MIX - D_ATT
N_ATT_HEADS = 4
ATT_HEAD_DIM = D_ATT // (2 * N_ATT_HEADS)
ATT_V_DIM = 2 * ATT_HEAD_DIM
N_RNN_BLOCKS = 8
RNN_BLOCK = D_RNN // N_RNN_BLOCKS
CONV_WIDTH = 4
RGLRU_C = 8.0
D_FF = 2816
N_SUB = 3
W_IN_COLS = 3 * D_ATT + 2 * D_RNN
EPS = 1e-6

kernel_name = "hybrid_diffattn_rglru_macaron_adaln"


def _rmsnorm(x, g):
    xf = x.astype(jnp.float32)
    y = xf * lax.rsqrt(jnp.mean(xf * xf, axis=-1, keepdims=True) + EPS)
    return (y * g.astype(jnp.float32)).astype(x.dtype)


def _modulate(h, shift, scale):
    return h * (1.0 + scale[:, None, :]) + shift[:, None, :]


def _swiglu(h, w_gate, w_up, w_down):
    return (jax.nn.silu(h @ w_gate) * (h @ w_up)) @ w_down


def _alibi_slopes(n_heads):
    return 2.0 ** (-8.0 * (jnp.arange(n_heads, dtype=jnp.float32) + 1.0) / n_heads)


def _diff_attention(q, k, v, lam):
    S = q.shape[1]
    scale = ATT_HEAD_DIM ** -0.5
    slopes = _alibi_slopes(N_ATT_HEADS)
    outs = []
    for blk in range(S // Q_BLOCK):
        q0 = blk * Q_BLOCK
        kend = q0 + Q_BLOCK
        qb = q[:, q0:kend]
        kb = k[:, :kend]
        vb = v[:, :kend]
        s = jnp.einsum('bqhmd,bkhmd->bhmqk', qb, kb,
                       preferred_element_type=jnp.float32) * scale
        tq = jnp.arange(q0, kend)
        tk = jnp.arange(kend)
        dist = jnp.abs(tq[:, None] - tk[None, :]).astype(jnp.float32)
        allowed = (tk[None, :] // CHUNK) <= (tq[:, None] // CHUNK)
        s = s - slopes[None, :, None, None, None] * dist[None, None, None]
        s = jnp.where(allowed[None, None, None], s, -jnp.inf)
        p = jax.nn.softmax(s, axis=-1)
        a = p[:, :, 0] - lam * p[:, :, 1]
        outs.append(jnp.einsum('bhqk,bkhe->bqhe', a.astype(v.dtype), vb))
    return jnp.concatenate(outs, axis=1)


def _causal_depthwise_conv(x, w, b):
    S = x.shape[1]
    xp = jnp.pad(x, ((0, 0), (CONV_WIDTH - 1, 0), (0, 0)))
    y = xp[:, 0:S] * w[0]
    for i in range(1, CONV_WIDTH):
        y = y + xp[:, i:i + S] * w[i]
    return y + b


def _rglru(x, w_a, b_a, w_x, b_x, lru_lambda):
    B, S, _ = x.shape
    xf = x.astype(jnp.float32)
    xb = xf.reshape(B, S, N_RNN_BLOCKS, RNN_BLOCK)
    r = jax.nn.sigmoid(jnp.einsum('bsnc,ncd->bsnd', xb, w_a.astype(jnp.float32)).reshape(B, S, D_RNN)
                       + b_a.astype(jnp.float32))
    i = jax.nn.sigmoid(jnp.einsum('bsnc,ncd->bsnd', xb, w_x.astype(jnp.float32)).reshape(B, S, D_RNN)
                       + b_x.astype(jnp.float32))
    log_a = -RGLRU_C * r * jax.nn.softplus(-lru_lambda.astype(jnp.float32))
    a = jnp.exp(log_a)
    u = jnp.sqrt(-jnp.expm1(2.0 * log_a)) * (i * xf)

    def combine(left, right):
        a_l, u_l = left
        a_r, u_r = right
        return a_l * a_r, a_r * u_l + u_r

    _, h = lax.associative_scan(combine, (a, u), axis=1)
    return h.astype(x.dtype)


def setup_inputs(seed: int = 0) -> dict:
    key = jax.random.key(seed)
    ks = jax.random.split(key, 32)
    f32 = jnp.float32
    L = DEPTH
    nrm = lambda k, shape, s: jax.random.normal(k, shape, f32) * s
    u = jax.random.uniform(ks[20], (L, D_RNN), f32, 0.9, 0.999)
    sig_l = u ** (1.0 / RGLRU_C)
    lru_lambda = jnp.log(sig_l) - jnp.log1p(-sig_l)
    return {
        "x": nrm(ks[0], (BATCH, SEQ, D_MODEL), 1.0),
        "c": nrm(ks[1], (BATCH, D_MODEL), 1.0),
        "w_ada": nrm(ks[2], (L, D_MODEL, N_SUB * 3 * D_MODEL), 0.5 * D_MODEL ** -0.5),
        "b_ada": nrm(ks[3], (L, N_SUB * 3 * D_MODEL), 0.01),
        "g_norm": 1.0 + nrm(ks[4], (L, N_SUB, D_MODEL), 0.01),
        "ffn1_w_gate": nrm(ks[5], (L, D_MODEL, D_FF), D_MODEL ** -0.5),
        "ffn1_w_up": nrm(ks[6], (L, D_MODEL, D_FF), D_MODEL ** -0.5),
        "ffn1_w_down": nrm(ks[7], (L, D_FF, D_MODEL), D_FF ** -0.5),
        "w_in": nrm(ks[8], (L, D_MODEL, W_IN_COLS), D_MODEL ** -0.5),
        "q_norm_g": 1.0 + nrm(ks[9], (L, 2, ATT_HEAD_DIM), 0.01),
        "k_norm_g": 1.0 + nrm(ks[10], (L, 2, ATT_HEAD_DIM), 0.01),
        "lambda_params": nrm(ks[11], (L, 4, ATT_HEAD_DIM), 0.1),
        "head_norm_g": 1.0 + nrm(ks[12], (L, ATT_V_DIM), 0.01),
        "conv_w": nrm(ks[13], (L, CONV_WIDTH, D_RNN), CONV_WIDTH ** -0.5),
        "conv_b": nrm(ks[14], (L, D_RNN), 0.01),
        "w_rg_a": nrm(ks[15], (L, N_RNN_BLOCKS, RNN_BLOCK, RNN_BLOCK), RNN_BLOCK ** -0.5),
        "b_rg_a": nrm(ks[16], (L, D_RNN), 0.01),
        "w_rg_x": nrm(ks[17], (L, N_RNN_BLOCKS, RNN_BLOCK, RNN_BLOCK), RNN_BLOCK ** -0.5),
        "b_rg_x": nrm(ks[18], (L, D_RNN), 0.01),
        "lru_lambda": lru_lambda,
        "w_out": nrm(ks[19], (L, D_MIX, D_MODEL), D_MIX ** -0.5),
        "ffn2_w_gate": nrm(ks[21], (L, D_MODEL, D_FF), D_MODEL ** -0.5),
        "ffn2_w_up": nrm(ks[22], (L, D_MODEL, D_FF), D_MODEL ** -0.5),
        "ffn2_w_down": nrm(ks[23], (L, D_FF, D_MODEL), D_FF ** -0.5),
    }


def reference(x, c, w_ada, b_ada, g_norm, ffn1_w_gate, ffn1_w_up, ffn1_w_down,
              w_in, q_norm_g, k_norm_g, lambda_params, head_norm_g, conv_w, conv_b,
              w_rg_a, b_rg_a, w_rg_x, b_rg_x, lru_lambda, w_out,
              ffn2_w_gate, ffn2_w_up, ffn2_w_down):
    B, S, D = x.shape
    c_act = jax.nn.silu(c)
    for l in range(DEPTH):
        mod = (c_act @ w_ada[l] + b_ada[l]).reshape(B, N_SUB, 3, D)

        h = _modulate(_rmsnorm(x, g_norm[l, 0]), mod[:, 0, 0], mod[:, 0, 1])
        x = x + 0.5 * mod[:, 0, 2][:, None, :] * _swiglu(h, ffn1_w_gate[l], ffn1_w_up[l], ffn1_w_down[l])

        h = _modulate(_rmsnorm(x, g_norm[l, 1]), mod[:, 1, 0], mod[:, 1, 1])
        proj = h @ w_in[l]
        q = proj[..., 0:D_ATT].reshape(B, S, N_ATT_HEADS, 2, ATT_HEAD_DIM)
        k = proj[..., D_ATT:2 * D_ATT].reshape(B, S, N_ATT_HEADS, 2, ATT_HEAD_DIM)
        v = proj[..., 2 * D_ATT:3 * D_ATT].reshape(B, S, N_ATT_HEADS, ATT_V_DIM)
        gate_br = proj[..., 3 * D_ATT:3 * D_ATT + D_RNN]
        x_br = proj[..., 3 * D_ATT + D_RNN:]

        q = _rmsnorm(q, q_norm_g[l])
        k = _rmsnorm(k, k_norm_g[l])
        lam_init = 0.8 - 0.6 * math.exp(-0.3 * l)
        lp = lambda_params[l].astype(jnp.float32)
        lam = jnp.exp(jnp.sum(lp[0] * lp[1])) - jnp.exp(jnp.sum(lp[2] * lp[3])) + lam_init
        att = _diff_attention(q, k, v, lam)
        att = (_rmsnorm(att, head_norm_g[l]) * (1.0 - lam_init)).reshape(B, S, D_ATT)

        xr = _causal_depthwise_conv(x_br, conv_w[l], conv_b[l])
        hr = _rglru(xr, w_rg_a[l], b_rg_a[l], w_rg_x[l], b_rg_x[l], lru_lambda[l])
        rnn = jax.nn.gelu(gate_br) * hr

        mix = jnp.concatenate([att, rnn], axis=-1) @ w_out[l]
        x = x + mod[:, 1, 2][:, None, :] * mix

        h = _modulate(_rmsnorm(x, g_norm[l, 2]), mod[:, 2, 0], mod[:, 2, 1])
        x = x + 0.5 * mod[:, 2, 2][:, None, :] * _swiglu(h, ffn2_w_gate[l], ffn2_w_up[l], ffn2_w_down[l])
    return x
```

```python
import functools
import math

import jax
import jax.numpy as jnp
from jax import lax
from jax.experimental import pallas as pl
from jax.experimental.pallas import tpu as pltpu

F32 = jnp.float32
BF16 = jnp.bfloat16

EPS = 1e-6
CHUNK = 64
N_HEADS = 4
HEAD_W = 128
MAP_W = 64
RNN_BLOCK = 64
CONV_WIDTH = 4
RGLRU_C = 8.0
LAM_INIT = 0.8 - 0.6 * math.exp(-0.3 * 0)
ALIBI_SLOPES = tuple(2.0 ** (-8.0 * (h + 1) / N_HEADS) for h in range(N_HEADS))
MASK_VALUE = -1e30

TOKEN_TILE = 512
ATT_TILE = 256
MXU_TILE = 256
ADALN_COLS = 1024
VMEM_LIMIT_BYTES = 56 * 1024 * 1024


def _params(*semantics):
    return pltpu.CompilerParams(dimension_semantics=semantics, vmem_limit_bytes=VMEM_LIMIT_BYTES)


def _resident(shape):
    zeros = (0,) * len(shape)
    return pl.BlockSpec(shape, lambda *_: zeros, pipeline_mode=pl.Buffered(1))


def _split_bf16(x):
    hi = x.astype(BF16)
    lo = (x - hi.astype(F32)).astype(BF16)
    return hi, lo


def _dot(a, b):
    return jnp.dot(a, b, preferred_element_type=F32)


def _norm_mod(x, g, shift, scale):
    y = x * lax.rsqrt(jnp.mean(x * x, axis=-1, keepdims=True) + EPS)
    return (y * g) * (1.0 + scale) + shift


def _adaln_kernel(c_ref, w_ref, b_ref, o_ref):
    c = c_ref[...]
    c_act = c * jax.nn.sigmoid(c)
    c_hi, c_lo = _split_bf16(c_act)
    w_hi, w_lo = _split_bf16(w_ref[...])
    o_ref[...] = (_dot(c_hi, w_hi) + (_dot(c_hi, w_lo) + _dot(c_lo, w_hi))) + b_ref[...]


def _adaln(c, w, b):
    bsz, d = c.shape
    n = w.shape[1]
    return pl.pallas_call(
        _adaln_kernel,
        out_shape=jax.ShapeDtypeStruct((bsz, n), F32),
        grid=(n // ADALN_COLS,),
        in_specs=[
            pl.BlockSpec((bsz, d), lambda j: (0, 0)),
            pl.BlockSpec((d, ADALN_COLS), lambda j: (0, j)),
            pl.BlockSpec((1, ADALN_COLS), lambda j: (0, j)),
        ],
        out_specs=pl.BlockSpec((bsz, ADALN_COLS), lambda j: (0, j)),
        compiler_params=_params("parallel"),
        name="adaln",
    )(c, w, b.reshape(1, n))


def _ffn_kernel(x_ref, mod_ref, gn_ref, wg_ref, wu_ref, wd_ref, o_ref, *, sub):
    x = x_ref[...]
    shift = mod_ref[0, 3 * sub:3 * sub + 1, :]
    scale = mod_ref[0, 3 * sub + 1:3 * sub + 2, :]
    gate = mod_ref[0, 3 * sub + 2:3 * sub + 3, :]
    h = _norm_mod(x, gn_ref[sub:sub + 1, :], shift, scale).astype(BF16)
    g = _dot(h, wg_ref[...])
    u = _dot(h, wu_ref[...])
    a = ((g * jax.nn.sigmoid(g)) * u).astype(BF16)
    o_ref[...] = x + (0.5 * gate) * _dot(a, wd_ref[...])


def _ffn(x2d, mod, g_norm, wg, wu, wd, *, sub, seq):
    n, d = x2d.shape
    f = wg.shape[1]
    tiles_per_seq = seq // TOKEN_TILE
    return pl.pallas_call(
        functools.partial(_ffn_kernel, sub=sub),
        out_shape=jax.ShapeDtypeStruct((n, d), F32),
        grid=(n // TOKEN_TILE,),
        in_specs=[
            pl.BlockSpec((TOKEN_TILE, d), lambda i: (i, 0)),
            pl.BlockSpec((1,) + mod.shape[1:], lambda i: (i // tiles_per_seq, 0, 0)),
            _resident(g_norm.shape),
            _resident((d, f)),
            _resident((d, f)),
            _resident((f, d)),
        ],
        out_specs=pl.BlockSpec((TOKEN_TILE, d), lambda i: (i, 0)),
        compiler_params=_params("parallel"),
        name=f"ffn{sub}",
    )(x2d, mod, g_norm, wg, wu, wd)


def _group_mean_sq(x, ones_ref):
    hi, lo = _split_bf16(x * x)
    ones = ones_ref[...]
    cols = []
    for c in range(0, x.shape[1], MXU_TILE):
        cols.append(_dot(hi[:, c:c + MXU_TILE], ones) + _dot(lo[:, c:c + MXU_TILE], ones))
    return jnp.concatenate(cols, axis=1) * (1.0 / MAP_W)


def _inproj_kernel(x_ref, mod_ref, gn_ref, w_ref, ones_ref, qg_ref, kg_ref,
                   qt_ref, k_ref, vt_ref, gate_ref, xbr_ref):
    x = x_ref[...]
    h = _norm_mod(x, gn_ref[1:2, :], mod_ref[0, 3:4, :], mod_ref[0, 4:5, :]).astype(BF16)
    proj = _dot(h, w_ref[...])
    d_att = N_HEADS * HEAD_W
    q = proj[:, 0:d_att]
    k = proj[:, d_att:2 * d_att]
    v = proj[:, 2 * d_att:3 * d_att]
    gate_ref[...] = proj[:, 3 * d_att:4 * d_att]
    xbr_ref[...] = proj[:, 4 * d_att:5 * d_att]

    qn = (q * lax.rsqrt(_group_mean_sq(q, ones_ref) + EPS)) * qg_ref[...] * (MAP_W ** -0.5)
    kn = (k * lax.rsqrt(_group_mean_sq(k, ones_ref) + EPS)) * kg_ref[...]
    k_ref[...] = kn.astype(BF16)
    tm = x.shape[0]
    for hd in range(N_HEADS):
        lanes = slice(hd * HEAD_W, (hd + 1) * HEAD_W)
        qt_ref[0, hd] = qn[:, lanes].T.astype(BF16)
        for c in range(tm // ATT_TILE):
            rows = slice(c * ATT_TILE, (c + 1) * ATT_TILE)
            vt_ref[0, hd, c] = v[rows, lanes].T.astype(BF16)


def _inproj(x2d, mod, g_norm, w_in, ones, qg, kg, *, bsz, seq):
    n, d = x2d.shape
    cols = w_in.shape[1]
    d_att = N_HEADS * HEAD_W
    tiles_per_seq = seq // TOKEN_TILE
    blocks_per_tile = TOKEN_TILE // ATT_TILE
    row_block = pl.BlockSpec((TOKEN_TILE, d_att), lambda i: (i, 0))
    return pl.pallas_call(
        _inproj_kernel,
        out_shape=(
            jax.ShapeDtypeStruct((bsz, N_HEADS, HEAD_W, seq), BF16),
            jax.ShapeDtypeStruct((n, d_att), BF16),
            jax.ShapeDtypeStruct((bsz, N_HEADS, seq // ATT_TILE, HEAD_W, ATT_TILE), BF16),
            jax.ShapeDtypeStruct((n, d_att), F32),
            jax.ShapeDtypeStruct((n, d_att), F32),
        ),
        grid=(n // TOKEN_TILE,),
        in_specs=[
            pl.BlockSpec((TOKEN_TILE, d), lambda i: (i, 0)),
            pl.BlockSpec((1,) + mod.shape[1:], lambda i: (i // tiles_per_seq, 0, 0)),
            _resident(g_norm.shape),
            _resident((d, cols)),
            _resident(ones.shape),
            _resident(qg.shape),
            _resident(kg.shape),
        ],
        out_specs=(
            pl.BlockSpec((1, N_HEADS, HEAD_W, TOKEN_TILE),
                         lambda i: (i // tiles_per_seq, 0, 0, i % tiles_per_seq)),
            row_block,
            pl.BlockSpec((1, N_HEADS, blocks_per_tile, HEAD_W, ATT_TILE),
                         lambda i: (i // tiles_per_seq, 0, i % tiles_per_seq, 0, 0)),
            row_block,
            row_block,
        ),
        compiler_params=_params("parallel"),
        name="inproj",
    )(x2d, mod, g_norm, w_in, ones, qg, kg)


def _attn_kernel(lp_ref, hg_ref, qt_ref, k_ref, vt_ref, o_ref, m_ref, l_ref, acc_ref, bias_ref):
    t = ATT_TILE
    i = pl.program_id(1)
    lp = lp_ref[...]
    lam = (jnp.exp(jnp.sum(lp[0:1] * lp[1:2], axis=-1, keepdims=True))
           - jnp.exp(jnp.sum(lp[2:3] * lp[3:4], axis=-1, keepdims=True)) + LAM_INIT)

    key_pos = lax.broadcasted_iota(jnp.int32, (t, 2 * t), 0)
    lane = lax.broadcasted_iota(jnp.int32, (t, 2 * t), 1)
    qry_pos = jnp.where(lane >= t, lane - t, lane)
    map_row = lax.broadcasted_iota(jnp.int32, (HEAD_W, t), 0)

    qq = []
    for hd in range(N_HEADS):
        qt = qt_ref[0, hd]
        zero = jnp.zeros_like(qt)
        qq.append(jnp.concatenate([jnp.where(map_row < MAP_W, qt, zero),
                                   jnp.where(map_row >= MAP_W, qt, zero)], axis=1))
        bias_ref[hd] = ALIBI_SLOPES[hd] * key_pos.astype(F32)
        m_ref[hd] = jnp.full((1, 2 * t), MASK_VALUE, F32)
        l_ref[hd] = jnp.zeros((1, 2 * t), F32)
        acc_ref[hd] = jnp.zeros((HEAD_W, 2 * t), F32)

    def update(hd, s, vt, shift):
        m_old = m_ref[hd]
        m_new = jnp.maximum(m_old, jnp.max(s, axis=0, keepdims=True) - shift)
        alpha = jnp.exp(m_old - m_new)
        p = jnp.exp(s - (m_new + shift))
        l_ref[hd] = alpha * l_ref[hd] + jnp.sum(p, axis=0, keepdims=True)
        acc_ref[hd] = alpha * acc_ref[hd] + _dot(vt, p.astype(BF16))
        m_ref[hd] = m_new

    def visible_block(j, carry):
        start = pl.multiple_of(j * t, t)
        dist = ((i - j) * t).astype(F32)
        for hd in range(N_HEADS):
            kj = k_ref[0, pl.ds(start, t), hd * HEAD_W:(hd + 1) * HEAD_W]
            s = _dot(kj, qq[hd]) + bias_ref[hd]
            update(hd, s, vt_ref[0, hd, j], ALIBI_SLOPES[hd] * dist)
        return carry

    lax.fori_loop(0, i, visible_block, 0)

    start = pl.multiple_of(i * t, t)
    chunk_shift = CHUNK.bit_length() - 1
    allowed = (key_pos >> chunk_shift) <= (qry_pos >> chunk_shift)
    rel = (qry_pos - jnp.abs(qry_pos - key_pos)).astype(F32)
    for hd in range(N_HEADS):
        kj = k_ref[0, pl.ds(start, t), hd * HEAD_W:(hd + 1) * HEAD_W]
        s = jnp.where(allowed, _dot(kj, qq[hd]) + ALIBI_SLOPES[hd] * rel, MASK_VALUE)
        update(hd, s, vt_ref[0, hd, i], 0.0)

    for hd in range(N_HEADS):
        o = acc_ref[hd] * (1.0 / l_ref[hd])
        diff = o[:, :t] - lam * o[:, t:]
        y = diff * lax.rsqrt(jnp.mean(diff * diff, axis=0, keepdims=True) + EPS)
        y = (y.T * hg_ref[...]) * (1.0 - LAM_INIT)
        o_ref[0, :, hd * HEAD_W:(hd + 1) * HEAD_W] = y.astype(BF16)


def _attention(lp, hg, qt, k3, vt):
    bsz, _, _, seq = qt.shape
    t = ATT_TILE
    d_att = N_HEADS * HEAD_W
    return pl.pallas_call(
        _attn_kernel,
        out_shape=jax.ShapeDtypeStruct((bsz, seq, d_att), BF16),
        grid=(bsz, seq // t),
        in_specs=[
            _resident(lp.shape),
            _resident(hg.shape),
            pl.BlockSpec((1, N_HEADS, HEAD_W, t), lambda b, i: (b, 0, 0, i)),
            pl.BlockSpec((1, seq, d_att), lambda b, i: (b, 0, 0)),
            pl.BlockSpec((1, N_HEADS, seq // t, HEAD_W, t), lambda b, i: (b, 0, 0, 0, 0)),
        ],
        out_specs=pl.BlockSpec((1, t, d_att), lambda b, i: (b, i, 0)),
        scratch_shapes=[
            pltpu.VMEM((N_HEADS, 1, 2 * t), F32),
            pltpu.VMEM((N_HEADS, 1, 2 * t), F32),
            pltpu.VMEM((N_HEADS, HEAD_W, 2 * t), F32),
            pltpu.VMEM((N_HEADS, t, 2 * t), F32),
        ],
        compiler_params=_params("parallel", "arbitrary"),
        name="diff_attention",
    )(lp, hg, qt, k3, vt)


def _block_diag_dot(x, w_ref):
    cols = []
    for c in range(0, x.shape[1], MXU_TILE):
        cols.append(_dot(x[:, c:c + MXU_TILE], w_ref[c:c + MXU_TILE, c:c + MXU_TILE]))
    return jnp.concatenate(cols, axis=1)


def _mixout_kernel(x_ref, mod_ref, att_ref, gate_ref, xbr_ref, cw_ref, cb_ref, wa_ref, ba_ref,
                   wx_ref, bx_ref, lam_ref, wo_ref, o_ref, xbuf_ref, h_ref):
    tm = x_ref.shape[1]
    pad = 8
    d_att = att_ref.shape[2]

    @pl.when(pl.program_id(1) == 0)
    def _():
        xbuf_ref[0:pad, :] = jnp.zeros((pad, xbuf_ref.shape[1]), F32)
        h_ref[...] = jnp.zeros_like(h_ref)

    xb = xbr_ref[0]
    xbuf_ref[pad:pad + tm, :] = xb
    xr = xbuf_ref[pad - 3:pad - 3 + tm, :] * cw_ref[0:1, :]
    for tap in range(1, CONV_WIDTH):
        off = pad - (CONV_WIDTH - 1) + tap
        xr = xr + xbuf_ref[off:off + tm, :] * cw_ref[tap:tap + 1, :]
    xr = xr + cb_ref[...]
    xbuf_ref[0:pad, :] = xb[tm - pad:tm, :]

    xr16 = xr.astype(BF16)
    r = jax.nn.sigmoid(_block_diag_dot(xr16, wa_ref) + ba_ref[...])
    ig = jax.nn.sigmoid(_block_diag_dot(xr16, wx_ref) + bx_ref[...])
    neg = -lam_ref[...]
    softplus = jnp.maximum(neg, 0.0) + jnp.log1p(jnp.exp(-jnp.abs(neg)))
    log_a = (-RGLRU_C * r) * softplus
    a = jnp.exp(log_a)
    u = jnp.sqrt(1.0 - jnp.exp(2.0 * log_a)) * (ig * xr)

    row = lax.broadcasted_iota(jnp.int32, a.shape, 0)
    step = 1
    while step < tm:
        valid = row >= step
        a_prev = jnp.where(valid, pltpu.roll(a, step, 0), 1.0)
        u_prev = jnp.where(valid, pltpu.roll(u, step, 0), 0.0)
        u = a * u_prev + u
        a = a * a_prev
        step *= 2
    h = u + a * h_ref[...]
    h_ref[...] = h[tm - 1:tm, :]

    rnn = (jax.nn.gelu(gate_ref[0]) * h).astype(BF16)
    mix = _dot(att_ref[0], wo_ref[0:d_att, :]) + _dot(rnn, wo_ref[d_att:, :])
    o_ref[0] = x_ref[0] + mod_ref[0, 5:6, :] * mix


def _mixout(x3, mod, att, gate3, xbr3, cw, cb, wa, ba, wx, bx, lam, wo):
    bsz, seq, d = x3.shape
    d_att = att.shape[2]
    d_rnn = xbr3.shape[2]
    tm = TOKEN_TILE

    def rows(width):
        return pl.BlockSpec((1, tm, width), lambda b, i: (b, i, 0))

    return pl.pallas_call(
        _mixout_kernel,
        out_shape=jax.ShapeDtypeStruct((bsz, seq, d), F32),
        grid=(bsz, seq // tm),
        in_specs=[
            rows(d),
            pl.BlockSpec((1,) + mod.shape[1:], lambda b, i: (b, 0, 0)),
            rows(d_att), rows(d_rnn), rows(d_rnn),
            _resident(cw.shape), _resident(cb.shape),
            _resident(wa.shape), _resident(ba.shape),
            _resident(wx.shape), _resident(bx.shape),
            _resident(lam.shape), _resident(wo.shape),
        ],
        out_specs=rows(d),
        scratch_shapes=[
            pltpu.VMEM((tm + 8, d_rnn), F32),
            pltpu.VMEM((1, d_rnn), F32),
        ],
        compiler_params=_params("arbitrary", "arbitrary"),
        name="mixout",
    )(x3, mod, att, gate3, xbr3, cw, cb, wa, ba, wx, bx, lam, wo)


def _dense_block_diag(w):
    nb, r, _ = w.shape
    eye = jnp.eye(nb, dtype=w.dtype)
    return jnp.einsum("ncd,nm->ncmd", w, eye).reshape(nb * r, nb * r)


def kernel(x, c, w_ada, b_ada, g_norm, ffn1_w_gate, ffn1_w_up, ffn1_w_down, w_in, q_norm_g, k_norm_g,
           lambda_params, head_norm_g, conv_w, conv_b, w_rg_a, b_rg_a, w_rg_x, b_rg_x, lru_lambda, w_out,
           ffn2_w_gate, ffn2_w_up, ffn2_w_down):
    bsz, seq, d = x.shape
    depth = w_ada.shape[0]
    assert seq % TOKEN_TILE == 0 and TOKEN_TILE % ATT_TILE == 0 and ATT_TILE % CHUNK == 0
    assert w_in.shape[2] == 5 * N_HEADS * HEAD_W and q_norm_g.shape[1:] == (2, MAP_W)
    assert w_rg_a.shape[2] == RNN_BLOCK and MXU_TILE % RNN_BLOCK == 0 and conv_w.shape[1] == CONV_WIDTH
    assert depth == 1, "LAM_INIT is fixed for layer 0"
    assert CHUNK & (CHUNK - 1) == 0

    ones = _dense_block_diag(jnp.ones((MXU_TILE // MAP_W, MAP_W, MAP_W), BF16))
    for l in range(depth):
        mod = _adaln(c, w_ada[l], b_ada[l]).reshape(bsz, 9, d)
        gn = g_norm[l]

        x2d = _ffn(x.reshape(bsz * seq, d), mod, gn, ffn1_w_gate[l].astype(BF16),
                   ffn1_w_up[l].astype(BF16), ffn1_w_down[l].astype(BF16), sub=0, seq=seq)

        qg = jnp.tile(q_norm_g[l].reshape(1, HEAD_W), (1, N_HEADS))
        kg = jnp.tile(k_norm_g[l].reshape(1, HEAD_W), (1, N_HEADS))
        qt, k2d, vt, gate2d, xbr2d = _inproj(x2d, mod, gn, w_in[l].astype(BF16), ones, qg, kg,
                                             bsz=bsz, seq=seq)
        d_att = k2d.shape[1]
        att = _attention(lambda_params[l], head_norm_g[l].reshape(1, HEAD_W), qt,
                         k2d.reshape(bsz, seq, d_att), vt)
        x3 = _mixout(x2d.reshape(bsz, seq, d), mod, att,
                     gate2d.reshape(bsz, seq, -1), xbr2d.reshape(bsz, seq, -1),
                     conv_w[l], conv_b[l].reshape(1, -1),
                     _dense_block_diag(w_rg_a[l]).astype(BF16), b_rg_a[l].reshape(1, -1),
                     _dense_block_diag(w_rg_x[l]).astype(BF16), b_rg_x[l].reshape(1, -1),
                     lru_lambda[l].reshape(1, -1), w_out[l].astype(BF16))

        x = _ffn(x3.reshape(bsz * seq, d), mod, gn, ffn2_w_gate[l].astype(BF16),
                 ffn2_w_up[l].astype(BF16), ffn2_w_down[l].astype(BF16), sub=2, seq=seq).reshape(bsz, seq, d)
    return x
```

```python
import functools
import math

import jax
import jax.numpy as jnp
import numpy as np
from jax import lax
from jax.experimental import pallas as pl
from jax.experimental.pallas import tpu as pltpu

F32 = jnp.float32
BF16 = jnp.bfloat16

EPS = 1e-6
CHUNK = 64
N_HEADS = 4
HEAD_W = 128
MAP_W = 64
RNN_BLOCK = 64
CONV_WIDTH = 4
RGLRU_C = 8.0
LAM_INIT = 0.8 - 0.6 * math.exp(-0.3 * 0)
ALIBI_SLOPES = tuple(2.0 ** (-8.0 * (h + 1) / N_HEADS) for h in range(N_HEADS))
MASK_VALUE = -1e30
LOG2E = math.log2(math.e)
POS_SPLIT = 64
POS_PARTS = 3
VT_ROWS = HEAD_W + 16

TOKEN_TILE = 512
ATT_TILE = 256
MXU_TILE = 256
ADALN_COLS = 1024
VMEM_LIMIT_BYTES = 56 * 1024 * 1024


def _params(*semantics):
    return pltpu.CompilerParams(dimension_semantics=semantics, vmem_limit_bytes=VMEM_LIMIT_BYTES)


def _resident(shape):
    zeros = (0,) * len(shape)
    return pl.BlockSpec(shape, lambda *_: zeros, pipeline_mode=pl.Buffered(1))


def _split_bf16(x):
    hi = x.astype(BF16)
    lo = (x - hi.astype(F32)).astype(BF16)
    return hi, lo


def _bf16_parts(value):
    parts, rest = [], float(value)
    for _ in range(POS_PARTS):
        part = float(np.asarray(rest, dtype=BF16))
        parts.append(part)
        rest -= part
    return parts


def _dot(a, b):
    return jnp.dot(a, b, preferred_element_type=F32)


def _norm_mod(x, g, shift, scale):
    y = x * lax.rsqrt(jnp.mean(x * x, axis=-1, keepdims=True) + EPS)
    return (y * g) * (1.0 + scale) + shift


def _adaln_kernel(c_ref, w_ref, b_ref, o_ref):
    c = c_ref[...]
    c_act = c * jax.nn.sigmoid(c)
    c_hi, c_lo = _split_bf16(c_act)
    w_hi, w_lo = _split_bf16(w_ref[...])
    o_ref[...] = (_dot(c_hi, w_hi) + (_dot(c_hi, w_lo) + _dot(c_lo, w_hi))) + b_ref[...]


def _adaln(c, w, b):
    bsz, d = c.shape
    n = w.shape[1]
    return pl.pallas_call(
        _adaln_kernel,
        out_shape=jax.ShapeDtypeStruct((bsz, n), F32),
        grid=(n // ADALN_COLS,),
        in_specs=[
            pl.BlockSpec((bsz, d), lambda j: (0, 0)),
            pl.BlockSpec((d, ADALN_COLS), lambda j: (0, j)),
            pl.BlockSpec((1, ADALN_COLS), lambda j: (0, j)),
        ],
        out_specs=pl.BlockSpec((bsz, ADALN_COLS), lambda j: (0, j)),
        compiler_params=_params("parallel"),
        name="adaln",
    )(c, w, b.reshape(1, n))


def _ffn_kernel(x_ref, mod_ref, gn_ref, wg_ref, wu_ref, wd_ref, o_ref, *, sub):
    x = x_ref[...]
    shift = mod_ref[0, 3 * sub:3 * sub + 1, :]
    scale = mod_ref[0, 3 * sub + 1:3 * sub + 2, :]
    gate = mod_ref[0, 3 * sub + 2:3 * sub + 3, :]
    h = _norm_mod(x, gn_ref[sub:sub + 1, :], shift, scale).astype(BF16)
    g = _dot(h, wg_ref[...])
    u = _dot(h, wu_ref[...])
    a = ((g * jax.nn.sigmoid(g)) * u).astype(BF16)
    o_ref[...] = x + (0.5 * gate) * _dot(a, wd_ref[...])


def _ffn(x2d, mod, g_norm, wg, wu, wd, *, sub, seq):
    n, d = x2d.shape
    f = wg.shape[1]
    tiles_per_seq = seq // TOKEN_TILE
    return pl.pallas_call(
        functools.partial(_ffn_kernel, sub=sub),
        out_shape=jax.ShapeDtypeStruct((n, d), F32),
        grid=(n // TOKEN_TILE,),
        in_specs=[
            pl.BlockSpec((TOKEN_TILE, d), lambda i: (i, 0)),
            pl.BlockSpec((1,) + mod.shape[1:], lambda i: (i // tiles_per_seq, 0, 0)),
            _resident(g_norm.shape),
            _resident((d, f)),
            _resident((d, f)),
            _resident((f, d)),
        ],
        out_specs=pl.BlockSpec((TOKEN_TILE, d), lambda i: (i, 0)),
        compiler_params=_params("parallel"),
        name=f"ffn{sub}",
    )(x2d, mod, g_norm, wg, wu, wd)


def _group_mean_sq(x, ones_ref):
    hi, lo = _split_bf16(x * x)
    ones = ones_ref[...]
    cols = []
    for c in range(0, x.shape[1], MXU_TILE):
        cols.append(_dot(hi[:, c:c + MXU_TILE], ones) + _dot(lo[:, c:c + MXU_TILE], ones))
    return jnp.concatenate(cols, axis=1) * (1.0 / MAP_W)


def _inproj_kernel(x_ref, mod_ref, gn_ref, w_ref, ones_ref, qg_ref, kg_ref,
                   qt_ref, k_ref, vt_ref, gate_ref, xbr_ref):
    x = x_ref[...]
    h = _norm_mod(x, gn_ref[1:2, :], mod_ref[0, 3:4, :], mod_ref[0, 4:5, :]).astype(BF16)
    proj = _dot(h, w_ref[...])
    d_att = N_HEADS * HEAD_W
    q = proj[:, 0:d_att]
    k = proj[:, d_att:2 * d_att]
    v = proj[:, 2 * d_att:3 * d_att]
    gate_ref[...] = proj[:, 3 * d_att:4 * d_att]
    xbr_ref[...] = proj[:, 4 * d_att:5 * d_att]

    qn = (q * lax.rsqrt(_group_mean_sq(q, ones_ref) + EPS)) * qg_ref[...] * (MAP_W ** -0.5 * LOG2E)
    kn = (k * lax.rsqrt(_group_mean_sq(k, ones_ref) + EPS)) * kg_ref[...]
    k_ref[...] = kn.astype(BF16)
    tm = x.shape[0]
    for hd in range(N_HEADS):
        lanes = slice(hd * HEAD_W, (hd + 1) * HEAD_W)
        qt_ref[0, hd] = qn[:, lanes].T.astype(BF16)
        for c in range(tm // ATT_TILE):
            rows = slice(c * ATT_TILE, (c + 1) * ATT_TILE)
            vt_ref[0, hd, c, 0:HEAD_W, :] = v[rows, lanes].T.astype(BF16)
            vt_ref[0, hd, c, HEAD_W:, :] = jnp.ones((VT_ROWS - HEAD_W, ATT_TILE), BF16)


def _inproj(x2d, mod, g_norm, w_in, ones, qg, kg, *, bsz, seq):
    n, d = x2d.shape
    cols = w_in.shape[1]
    d_att = N_HEADS * HEAD_W
    tiles_per_seq = seq // TOKEN_TILE
    blocks_per_tile = TOKEN_TILE // ATT_TILE
    row_block = pl.BlockSpec((TOKEN_TILE, d_att), lambda i: (i, 0))
    return pl.pallas_call(
        _inproj_kernel,
        out_shape=(
            jax.ShapeDtypeStruct((bsz, N_HEADS, HEAD_W, seq), BF16),
            jax.ShapeDtypeStruct((n, d_att), BF16),
            jax.ShapeDtypeStruct((bsz, N_HEADS, seq // ATT_TILE, VT_ROWS, ATT_TILE), BF16),
            jax.ShapeDtypeStruct((n, d_att), F32),
            jax.ShapeDtypeStruct((n, d_att), F32),
        ),
        grid=(n // TOKEN_TILE,),
        in_specs=[
            pl.BlockSpec((TOKEN_TILE, d), lambda i: (i, 0)),
            pl.BlockSpec((1,) + mod.shape[1:], lambda i: (i // tiles_per_seq, 0, 0)),
            _resident(g_norm.shape),
            _resident((d, cols)),
            _resident(ones.shape),
            _resident(qg.shape),
            _resident(kg.shape),
        ],
        out_specs=(
            pl.BlockSpec((1, N_HEADS, HEAD_W, TOKEN_TILE),
                         lambda i: (i // tiles_per_seq, 0, 0, i % tiles_per_seq)),
            row_block,
            pl.BlockSpec((1, N_HEADS, blocks_per_tile, VT_ROWS, ATT_TILE),
                         lambda i: (i // tiles_per_seq, 0, i % tiles_per_seq, 0, 0)),
            row_block,
            row_block,
        ),
        compiler_params=_params("parallel"),
        name="inproj",
    )(x2d, mod, g_norm, w_in, ones, qg, kg)


def _attn_kernel(lp_ref, hg_ref, kpos_ref, qt_ref, k_ref, vt_ref, o_ref, m_ref, acc_ref, qq_ref, s_ref):
    t = ATT_TILE
    i = pl.program_id(1)
    lp = lp_ref[...]
    lam = (jnp.exp(jnp.sum(lp[0:1] * lp[1:2], axis=-1, keepdims=True))
           - jnp.exp(jnp.sum(lp[2:3] * lp[3:4], axis=-1, keepdims=True)) + LAM_INIT)

    map_row = lax.broadcasted_iota(jnp.int32, (HEAD_W, t), 0)
    feat_row = lax.broadcasted_iota(jnp.int32, (HEAD_W, 2 * t), 0)
    for hd in range(N_HEADS):
        qt = qt_ref[0, hd]
        zero = jnp.zeros_like(qt)
        qq_ref[hd, 0:HEAD_W, :] = jnp.concatenate(
            [jnp.where(map_row < MAP_W, qt, zero), jnp.where(map_row >= MAP_W, qt, zero)], axis=1)
        feat = jnp.zeros((HEAD_W, 2 * t), F32)
        for r, part in enumerate(_bf16_parts(ALIBI_SLOPES[hd] * LOG2E)):
            feat = jnp.where(feat_row == r, part * POS_SPLIT, feat)
            feat = jnp.where(feat_row == POS_PARTS + r, part, feat)
        qq_ref[hd, HEAD_W:, :] = feat.astype(BF16)
        m_ref[hd] = jnp.full((1, 2 * t), MASK_VALUE, F32)
        acc_ref[hd] = jnp.zeros(acc_ref.shape[1:], F32)

    def scores(j, hd):
        rows = pl.ds(pl.multiple_of(j * t, t), t)
        kj = jnp.concatenate([k_ref[0, rows, hd * HEAD_W:(hd + 1) * HEAD_W], kpos_ref[rows, :]], axis=1)
        return _dot(kj, qq_ref[hd])

    def update(hd, s, vt):
        m_old = m_ref[hd]
        m_new = jnp.maximum(m_old, jnp.max(s, axis=0, keepdims=True))
        alpha = jnp.exp2(m_old - m_new)
        p = jnp.exp2(s - m_new).astype(BF16)
        acc_ref[hd] = alpha * acc_ref[hd] + _dot(vt, p)
        m_ref[hd] = m_new

    def step(j, slot):
        for hd in range(N_HEADS):
            s_ref[1 - slot, hd] = scores(j + 1, hd)
            update(hd, s_ref[slot, hd], vt_ref[0, hd, j])

    for hd in range(N_HEADS):
        s_ref[0, hd] = scores(0, hd)

    def visible_pair(jj, carry):
        step(2 * jj, 0)
        step(2 * jj + 1, 1)
        return carry

    odd = i & 1
    lax.fori_loop(0, lax.shift_right_logical(i, 1), visible_pair, 0)

    @pl.when(odd == 1)
    def _():
        step(i - 1, 0)

    s = [s_ref[odd, hd] for hd in range(N_HEADS)]

    key_pos = lax.broadcasted_iota(jnp.int32, (t, 2 * t), 0)
    lane = lax.broadcasted_iota(jnp.int32, (t, 2 * t), 1)
    qry_pos = jnp.where(lane >= t, lane - t, lane)
    chunk_shift = CHUNK.bit_length() - 1
    allowed = (key_pos >> chunk_shift) <= (qry_pos >> chunk_shift)
    ahead = jnp.maximum(key_pos - qry_pos, 0).astype(F32)
    for hd in range(N_HEADS):
        s_diag = jnp.where(allowed, s[hd] - (2.0 * LOG2E * ALIBI_SLOPES[hd]) * ahead, MASK_VALUE)
        update(hd, s_diag, vt_ref[0, hd, i])

    for hd in range(N_HEADS):
        acc = acc_ref[hd]
        o = acc[0:HEAD_W] * (1.0 / acc[HEAD_W:HEAD_W + 1])
        diff = o[:, :t] - lam * o[:, t:]
        y = diff * lax.rsqrt(jnp.mean(diff * diff, axis=0, keepdims=True) + EPS)
        y = (y.T * hg_ref[...]) * (1.0 - LAM_INIT)
        o_ref[0, :, hd * HEAD_W:(hd + 1) * HEAD_W] = y.astype(BF16)


def _attention(lp, hg, kpos, qt, k3, vt):
    bsz, _, _, seq = qt.shape
    t = ATT_TILE
    d_att = N_HEADS * HEAD_W
    return pl.pallas_call(
        _attn_kernel,
        out_shape=jax.ShapeDtypeStruct((bsz, seq, d_att), BF16),
        grid=(bsz, seq // t),
        in_specs=[
            _resident(lp.shape),
            _resident(hg.shape),
            _resident(kpos.shape),
            pl.BlockSpec((1, N_HEADS, HEAD_W, t), lambda b, i: (b, 0, 0, i)),
            pl.BlockSpec((1, seq, d_att), lambda b, i: (b, 0, 0)),
            pl.BlockSpec((1,) + vt.shape[1:], lambda b, i: (b, 0, 0, 0, 0)),
        ],
        out_specs=pl.BlockSpec((1, t, d_att), lambda b, i: (b, i, 0)),
        scratch_shapes=[
            pltpu.VMEM((N_HEADS, 1, 2 * t), F32),
            pltpu.VMEM((N_HEADS, VT_ROWS, 2 * t), F32),
            pltpu.VMEM((N_HEADS, 2 * HEAD_W, 2 * t), BF16),
            pltpu.VMEM((2, N_HEADS, t, 2 * t), F32),
        ],
        compiler_params=_params("parallel", "arbitrary"),
        name="diff_attention",
    )(lp, hg, kpos, qt, k3, vt)


def _block_diag_dot(x, w_ref):
    cols = []
    for c in range(0, x.shape[1], MXU_TILE):
        cols.append(_dot(x[:, c:c + MXU_TILE], w_ref[c:c + MXU_TILE, c:c + MXU_TILE]))
    return jnp.concatenate(cols, axis=1)


def _mixout_kernel(x_ref, mod_ref, att_ref, gate_ref, xbr_ref, cw_ref, cb_ref, wa_ref, ba_ref,
                   wx_ref, bx_ref, lam_ref, wo_ref, o_ref, xbuf_ref, h_ref):
    tm = x_ref.shape[1]
    pad = 8
    d_att = att_ref.shape[2]

    @pl.when(pl.program_id(1) == 0)
    def _():
        xbuf_ref[0:pad, :] = jnp.zeros((pad, xbuf_ref.shape[1]), F32)
        h_ref[...] = jnp.zeros_like(h_ref)

    xb = xbr_ref[0]
    xbuf_ref[pad:pad + tm, :] = xb
    xr = xbuf_ref[pad - 3:pad - 3 + tm, :] * cw_ref[0:1, :]
    for tap in range(1, CONV_WIDTH):
        off = pad - (CONV_WIDTH - 1) + tap
        xr = xr + xbuf_ref[off:off + tm, :] * cw_ref[tap:tap + 1, :]
    xr = xr + cb_ref[...]
    xbuf_ref[0:pad, :] = xb[tm - pad:tm, :]

    xr16 = xr.astype(BF16)
    r = jax.nn.sigmoid(_block_diag_dot(xr16, wa_ref) + ba_ref[...])
    ig = jax.nn.sigmoid(_block_diag_dot(xr16, wx_ref) + bx_ref[...])
    neg = -lam_ref[...]
    softplus = jnp.maximum(neg, 0.0) + jnp.log1p(jnp.exp(-jnp.abs(neg)))
    log_a = (-RGLRU_C * r) * softplus
    a = jnp.exp(log_a)
    u = jnp.sqrt(1.0 - jnp.exp(2.0 * log_a)) * (ig * xr)

    row = lax.broadcasted_iota(jnp.int32, a.shape, 0)
    step = 1
    while step < tm:
        valid = row >= step
        a_prev = jnp.where(valid, pltpu.roll(a, step, 0), 1.0)
        u_prev = jnp.where(valid, pltpu.roll(u, step, 0), 0.0)
        u = a * u_prev + u
        a = a * a_prev
        step *= 2
    h = u + a * h_ref[...]
    h_ref[...] = h[tm - 1:tm, :]

    rnn = (jax.nn.gelu(gate_ref[0]) * h).astype(BF16)
    mix = _dot(att_ref[0], wo_ref[0:d_att, :]) + _dot(rnn, wo_ref[d_att:, :])
    o_ref[0] = x_ref[0] + mod_ref[0, 5:6, :] * mix


def _mixout(x3, mod, att, gate3, xbr3, cw, cb, wa, ba, wx, bx, lam, wo):
    bsz, seq, d = x3.shape
    d_att = att.shape[2]
    d_rnn = xbr3.shape[2]
    tm = TOKEN_TILE

    def rows(width):
        return pl.BlockSpec((1, tm, width), lambda b, i: (b, i, 0))

    return pl.pallas_call(
        _mixout_kernel,
        out_shape=jax.ShapeDtypeStruct((bsz, seq, d), F32),
        grid=(bsz, seq // tm),
        in_specs=[
            rows(d),
            pl.BlockSpec((1,) + mod.shape[1:], lambda b, i: (b, 0, 0)),
            rows(d_att), rows(d_rnn), rows(d_rnn),
            _resident(cw.shape), _resident(cb.shape),
            _resident(wa.shape), _resident(ba.shape),
            _resident(wx.shape), _resident(bx.shape),
            _resident(lam.shape), _resident(wo.shape),
        ],
        out_specs=rows(d),
        scratch_shapes=[
            pltpu.VMEM((tm + 8, d_rnn), F32),
            pltpu.VMEM((1, d_rnn), F32),
        ],
        compiler_params=_params("arbitrary", "arbitrary"),
        name="mixout",
    )(x3, mod, att, gate3, xbr3, cw, cb, wa, ba, wx, bx, lam, wo)


def _dense_block_diag(w):
    nb, r, _ = w.shape
    eye = jnp.eye(nb, dtype=w.dtype)
    return jnp.einsum("ncd,nm->ncmd", w, eye).reshape(nb * r, nb * r)


def kernel(x, c, w_ada, b_ada, g_norm, ffn1_w_gate, ffn1_w_up, ffn1_w_down, w_in, q_norm_g, k_norm_g,
           lambda_params, head_norm_g, conv_w, conv_b, w_rg_a, b_rg_a, w_rg_x, b_rg_x, lru_lambda, w_out,
           ffn2_w_gate, ffn2_w_up, ffn2_w_down):
    bsz, seq, d = x.shape
    depth = w_ada.shape[0]
    assert seq % TOKEN_TILE == 0 and TOKEN_TILE % ATT_TILE == 0 and ATT_TILE % CHUNK == 0
    assert w_in.shape[2] == 5 * N_HEADS * HEAD_W and q_norm_g.shape[1:] == (2, MAP_W)
    assert w_rg_a.shape[2] == RNN_BLOCK and MXU_TILE % RNN_BLOCK == 0 and conv_w.shape[1] == CONV_WIDTH
    assert depth == 1, "LAM_INIT is fixed for layer 0"
    assert CHUNK & (CHUNK - 1) == 0

    assert seq <= POS_SPLIT * 256, "position parts must stay exact in bf16"
    ones = _dense_block_diag(jnp.ones((MXU_TILE // MAP_W, MAP_W, MAP_W), BF16))
    pos = lax.broadcasted_iota(jnp.int32, (seq, HEAD_W), 0)
    lane = lax.broadcasted_iota(jnp.int32, (seq, HEAD_W), 1)
    kpos = jnp.where(lane < POS_PARTS, pos // POS_SPLIT,
                     jnp.where(lane < 2 * POS_PARTS, pos % POS_SPLIT, 0)).astype(BF16)
    for l in range(depth):
        mod = _adaln(c, w_ada[l], b_ada[l]).reshape(bsz, 9, d)
        gn = g_norm[l]

        x2d = _ffn(x.reshape(bsz * seq, d), mod, gn, ffn1_w_gate[l].astype(BF16),
                   ffn1_w_up[l].astype(BF16), ffn1_w_down[l].astype(BF16), sub=0, seq=seq)

        qg = jnp.tile(q_norm_g[l].reshape(1, HEAD_W), (1, N_HEADS))
        kg = jnp.tile(k_norm_g[l].reshape(1, HEAD_W), (1, N_HEADS))
        qt, k2d, vt, gate2d, xbr2d = _inproj(x2d, mod, gn, w_in[l].astype(BF16), ones, qg, kg,
                                             bsz=bsz, seq=seq)
        d_att = k2d.shape[1]
        att = _attention(lambda_params[l], head_norm_g[l].reshape(1, HEAD_W), kpos, qt,
                         k2d.reshape(bsz, seq, d_att), vt)
        x3 = _mixout(x2d.reshape(bsz, seq, d), mod, att,
                     gate2d.reshape(bsz, seq, -1), xbr2d.reshape(bsz, seq, -1),
                     conv_w[l], conv_b[l].reshape(1, -1),
                     _dense_block_diag(w_rg_a[l]).astype(BF16), b_rg_a[l].reshape(1, -1),
                     _dense_block_diag(w_rg_x[l]).astype(BF16), b_rg_x[l].reshape(1, -1),
                     lru_lambda[l].reshape(1, -1), w_out[l].astype(BF16))

        x = _ffn(x3.reshape(bsz * seq, d), mod, gn, ffn2_w_gate[l].astype(BF16),
                 ffn2_w_up[l].astype(BF16), ffn2_w_down[l].astype(BF16), sub=2, seq=seq).reshape(bsz, seq, d)
    return x
```

```python
import functools
import math

import jax
import jax.numpy as jnp
import numpy as np
from jax import lax
from jax.experimental import pallas as pl
from jax.experimental.pallas import tpu as pltpu

F32 = jnp.float32
BF16 = jnp.bfloat16

EPS = 1e-6
CHUNK = 64
N_HEADS = 4
HEAD_W = 128
MAP_W = 64
RNN_BLOCK = 64
CONV_WIDTH = 4
RGLRU_C = 8.0
LAM_INIT = 0.8 - 0.6 * math.exp(-0.3 * 0)
ALIBI_SLOPES = tuple(2.0 ** (-8.0 * (h + 1) / N_HEADS) for h in range(N_HEADS))
MASK_VALUE = -1e30
LOG2E = math.log2(math.e)
POS_SPLIT = 64
POS_PARTS = 3
VT_ROWS = HEAD_W + 16

TOKEN_TILE = 512
MIX_SUBTILE = 256
ATT_TILE = 256
MXU_TILE = 256
SUBLANES = 8
ADALN_COLS = 1024
VMEM_LIMIT_BYTES = 56 * 1024 * 1024


def _params(*semantics):
    return pltpu.CompilerParams(dimension_semantics=semantics, vmem_limit_bytes=VMEM_LIMIT_BYTES)


def _resident(shape):
    zeros = (0,) * len(shape)
    return pl.BlockSpec(shape, lambda *_: zeros, pipeline_mode=pl.Buffered(1))


def _split_bf16(x):
    hi = x.astype(BF16)
    lo = (x - hi.astype(F32)).astype(BF16)
    return hi, lo


def _bf16_parts(value):
    parts, rest = [], float(value)
    for _ in range(POS_PARTS):
        part = float(np.asarray(rest, dtype=BF16))
        parts.append(part)
        rest -= part
    return parts


def _dot(a, b):
    return jnp.dot(a, b, preferred_element_type=F32)


def _norm_mod(x, g, shift, scale):
    y = x * lax.rsqrt(jnp.mean(x * x, axis=-1, keepdims=True) + EPS)
    return (y * g) * (1.0 + scale) + shift


def _adaln_kernel(c_ref, w_ref, b_ref, o_ref):
    c = c_ref[...]
    c_act = c * jax.nn.sigmoid(c)
    c_hi, c_lo = _split_bf16(c_act)
    w_hi, w_lo = _split_bf16(w_ref[...])
    o_ref[...] = (_dot(c_hi, w_hi) + (_dot(c_hi, w_lo) + _dot(c_lo, w_hi))) + b_ref[...]


def _adaln(c, w, b):
    bsz, d = c.shape
    n = w.shape[1]
    return pl.pallas_call(
        _adaln_kernel,
        out_shape=jax.ShapeDtypeStruct((bsz, n), F32),
        grid=(n // ADALN_COLS,),
        in_specs=[
            pl.BlockSpec((bsz, d), lambda j: (0, 0)),
            pl.BlockSpec((d, ADALN_COLS), lambda j: (0, j)),
            pl.BlockSpec((1, ADALN_COLS), lambda j: (0, j)),
        ],
        out_specs=pl.BlockSpec((bsz, ADALN_COLS), lambda j: (0, j)),
        compiler_params=_params("parallel"),
        name="adaln",
    )(c, w, b.reshape(1, n))


def _ffn_kernel(x_ref, mod_ref, gn_ref, wg_ref, wu_ref, wd_ref, o_ref, *, sub):
    x = x_ref[...]
    shift = mod_ref[0, 3 * sub:3 * sub + 1, :]
    scale = mod_ref[0, 3 * sub + 1:3 * sub + 2, :]
    gate = mod_ref[0, 3 * sub + 2:3 * sub + 3, :]
    h = _norm_mod(x, gn_ref[sub:sub + 1, :], shift, scale).astype(BF16)
    g = _dot(h, wg_ref[...])
    u = _dot(h, wu_ref[...])
    a = ((g * jax.nn.sigmoid(g)) * u).astype(BF16)
    o_ref[...] = x + (0.5 * gate) * _dot(a, wd_ref[...])


def _ffn(x2d, mod, g_norm, wg, wu, wd, *, sub, seq):
    n, d = x2d.shape
    f = wg.shape[1]
    tiles_per_seq = seq // TOKEN_TILE
    return pl.pallas_call(
        functools.partial(_ffn_kernel, sub=sub),
        out_shape=jax.ShapeDtypeStruct((n, d), F32),
        grid=(n // TOKEN_TILE,),
        in_specs=[
            pl.BlockSpec((TOKEN_TILE, d), lambda i: (i, 0)),
            pl.BlockSpec((1,) + mod.shape[1:], lambda i: (i // tiles_per_seq, 0, 0)),
            _resident(g_norm.shape),
            _resident((d, f)),
            _resident((d, f)),
            _resident((f, d)),
        ],
        out_specs=pl.BlockSpec((TOKEN_TILE, d), lambda i: (i, 0)),
        compiler_params=_params("parallel"),
        name=f"ffn{sub}",
    )(x2d, mod, g_norm, wg, wu, wd)


def _group_mean_sq(x, ones_ref):
    sq = (x * x).astype(BF16)
    ones = ones_ref[...]
    cols = [_dot(sq[:, c:c + MXU_TILE], ones) for c in range(0, x.shape[1], MXU_TILE)]
    return jnp.concatenate(cols, axis=1) * (1.0 / MAP_W)


def _inproj_kernel(x_ref, mod_ref, gn_ref, w_ref, ones_ref, qg_ref, kg_ref,
                   qt_ref, k_ref, vt_ref, gate_ref, xbr_ref):
    x = x_ref[...]
    h = _norm_mod(x, gn_ref[1:2, :], mod_ref[0, 3:4, :], mod_ref[0, 4:5, :]).astype(BF16)
    proj = _dot(h, w_ref[...])
    d_att = N_HEADS * HEAD_W
    q = proj[:, 0:d_att]
    k = proj[:, d_att:2 * d_att]
    v = proj[:, 2 * d_att:3 * d_att]
    gate_ref[...] = proj[:, 3 * d_att:4 * d_att]
    xbr_ref[...] = proj[:, 4 * d_att:5 * d_att]

    qn = (q * lax.rsqrt(_group_mean_sq(q, ones_ref) + EPS)) * qg_ref[...] * (MAP_W ** -0.5 * LOG2E)
    kn = (k * lax.rsqrt(_group_mean_sq(k, ones_ref) + EPS)) * kg_ref[...]
    k_ref[...] = kn.astype(BF16)
    tm = x.shape[0]
    for hd in range(N_HEADS):
        lanes = slice(hd * HEAD_W, (hd + 1) * HEAD_W)
        qt_ref[0, hd] = qn[:, lanes].T.astype(BF16)
        for c in range(tm // ATT_TILE):
            rows = slice(c * ATT_TILE, (c + 1) * ATT_TILE)
            vt_ref[0, hd, c, 0:HEAD_W, :] = v[rows, lanes].T.astype(BF16)
            vt_ref[0, hd, c, HEAD_W:, :] = jnp.ones((VT_ROWS - HEAD_W, ATT_TILE), BF16)


def _inproj(x2d, mod, g_norm, w_in, ones, qg, kg, *, bsz, seq):
    n, d = x2d.shape
    cols = w_in.shape[1]
    d_att = N_HEADS * HEAD_W
    tiles_per_seq = seq // TOKEN_TILE
    blocks_per_tile = TOKEN_TILE // ATT_TILE
    row_block = pl.BlockSpec((TOKEN_TILE, d_att), lambda i: (i, 0))
    return pl.pallas_call(
        _inproj_kernel,
        out_shape=(
            jax.ShapeDtypeStruct((bsz, N_HEADS, HEAD_W, seq), BF16),
            jax.ShapeDtypeStruct((n, d_att), BF16),
            jax.ShapeDtypeStruct((bsz, N_HEADS, seq // ATT_TILE, VT_ROWS, ATT_TILE), BF16),
            jax.ShapeDtypeStruct((n, d_att), F32),
            jax.ShapeDtypeStruct((n, d_att), F32),
        ),
        grid=(n // TOKEN_TILE,),
        in_specs=[
            pl.BlockSpec((TOKEN_TILE, d), lambda i: (i, 0)),
            pl.BlockSpec((1,) + mod.shape[1:], lambda i: (i // tiles_per_seq, 0, 0)),
            _resident(g_norm.shape),
            _resident((d, cols)),
            _resident(ones.shape),
            _resident(qg.shape),
            _resident(kg.shape),
        ],
        out_specs=(
            pl.BlockSpec((1, N_HEADS, HEAD_W, TOKEN_TILE),
                         lambda i: (i // tiles_per_seq, 0, 0, i % tiles_per_seq)),
            row_block,
            pl.BlockSpec((1, N_HEADS, blocks_per_tile, VT_ROWS, ATT_TILE),
                         lambda i: (i // tiles_per_seq, 0, i % tiles_per_seq, 0, 0)),
            row_block,
            row_block,
        ),
        compiler_params=_params("parallel"),
        name="inproj",
    )(x2d, mod, g_norm, w_in, ones, qg, kg)


def _attn_kernel(lp_ref, hg_ref, kpos_ref, qt_ref, k_ref, vt_ref, o_ref, m_ref, acc_ref, qq_ref, s_ref):
    t = ATT_TILE
    i = pl.program_id(1)
    lp = lp_ref[...]
    lam = (jnp.exp(jnp.sum(lp[0:1] * lp[1:2], axis=-1, keepdims=True))
           - jnp.exp(jnp.sum(lp[2:3] * lp[3:4], axis=-1, keepdims=True)) + LAM_INIT)

    map_row = lax.broadcasted_iota(jnp.int32, (HEAD_W, t), 0)
    feat_row = lax.broadcasted_iota(jnp.int32, (HEAD_W, 2 * t), 0)
    for hd in range(N_HEADS):
        qt = qt_ref[0, hd]
        zero = jnp.zeros_like(qt)
        qq_ref[hd, 0:HEAD_W, :] = jnp.concatenate(
            [jnp.where(map_row < MAP_W, qt, zero), jnp.where(map_row >= MAP_W, qt, zero)], axis=1)
        feat = jnp.zeros((HEAD_W, 2 * t), F32)
        for r, part in enumerate(_bf16_parts(ALIBI_SLOPES[hd] * LOG2E)):
            feat = jnp.where(feat_row == r, part * POS_SPLIT, feat)
            feat = jnp.where(feat_row == POS_PARTS + r, part, feat)
        qq_ref[hd, HEAD_W:, :] = feat.astype(BF16)
        m_ref[hd] = jnp.full((1, 2 * t), MASK_VALUE, F32)
        acc_ref[hd] = jnp.zeros(acc_ref.shape[1:], F32)

    def scores(j, hd):
        rows = pl.ds(pl.multiple_of(j * t, t), t)
        kj = jnp.concatenate([k_ref[0, rows, hd * HEAD_W:(hd + 1) * HEAD_W], kpos_ref[rows, :]], axis=1)
        return _dot(kj, qq_ref[hd])

    def update(hd, s, vt):
        m_old = m_ref[hd]
        m_new = jnp.maximum(m_old, jnp.max(s, axis=0, keepdims=True))
        alpha = jnp.exp2(m_old - m_new)
        p = jnp.exp2(s - m_new).astype(BF16)
        acc_ref[hd] = alpha * acc_ref[hd] + _dot(vt, p)
        m_ref[hd] = m_new

    def step(j, slot):
        for hd in range(N_HEADS):
            s_ref[1 - slot, hd] = scores(j + 1, hd)
            update(hd, s_ref[slot, hd], vt_ref[0, hd, j])

    for hd in range(N_HEADS):
        s_ref[0, hd] = scores(0, hd)

    def visible_pair(jj, carry):
        step(2 * jj, 0)
        step(2 * jj + 1, 1)
        return carry

    odd = i & 1
    lax.fori_loop(0, lax.shift_right_logical(i, 1), visible_pair, 0)

    @pl.when(odd == 1)
    def _():
        step(i - 1, 0)

    s = [s_ref[odd, hd] for hd in range(N_HEADS)]

    key_pos = lax.broadcasted_iota(jnp.int32, (t, 2 * t), 0)
    lane = lax.broadcasted_iota(jnp.int32, (t, 2 * t), 1)
    qry_pos = jnp.where(lane >= t, lane - t, lane)
    chunk_shift = CHUNK.bit_length() - 1
    allowed = (key_pos >> chunk_shift) <= (qry_pos >> chunk_shift)
    ahead = jnp.maximum(key_pos - qry_pos, 0).astype(F32)
    for hd in range(N_HEADS):
        s_diag = jnp.where(allowed, s[hd] - (2.0 * LOG2E * ALIBI_SLOPES[hd]) * ahead, MASK_VALUE)
        update(hd, s_diag, vt_ref[0, hd, i])

    for hd in range(N_HEADS):
        acc = acc_ref[hd]
        o = acc[0:HEAD_W] * (1.0 / acc[HEAD_W:HEAD_W + 1])
        diff = o[:, :t] - lam * o[:, t:]
        y = diff * lax.rsqrt(jnp.mean(diff * diff, axis=0, keepdims=True) + EPS)
        y = (y.T * hg_ref[...]) * (1.0 - LAM_INIT)
        o_ref[0, :, hd * HEAD_W:(hd + 1) * HEAD_W] = y.astype(BF16)


def _attention(lp, hg, kpos, qt, k3, vt):
    bsz, _, _, seq = qt.shape
    t = ATT_TILE
    d_att = N_HEADS * HEAD_W
    return pl.pallas_call(
        _attn_kernel,
        out_shape=jax.ShapeDtypeStruct((bsz, seq, d_att), BF16),
        grid=(bsz, seq // t),
        in_specs=[
            _resident(lp.shape),
            _resident(hg.shape),
            _resident(kpos.shape),
            pl.BlockSpec((1, N_HEADS, HEAD_W, t), lambda b, i: (b, 0, 0, i)),
            pl.BlockSpec((1, seq, d_att), lambda b, i: (b, 0, 0)),
            pl.BlockSpec((1,) + vt.shape[1:], lambda b, i: (b, 0, 0, 0, 0)),
        ],
        out_specs=pl.BlockSpec((1, t, d_att), lambda b, i: (b, i, 0)),
        scratch_shapes=[
            pltpu.VMEM((N_HEADS, 1, 2 * t), F32),
            pltpu.VMEM((N_HEADS, VT_ROWS, 2 * t), F32),
            pltpu.VMEM((N_HEADS, 2 * HEAD_W, 2 * t), BF16),
            pltpu.VMEM((2, N_HEADS, t, 2 * t), F32),
        ],
        compiler_params=_params("parallel", "arbitrary"),
        name="diff_attention",
    )(lp, hg, kpos, qt, k3, vt)


def _block_diag_dot(x, w_ref):
    cols = []
    for c in range(0, x.shape[1], MXU_TILE):
        cols.append(_dot(x[:, c:c + MXU_TILE], w_ref[c:c + MXU_TILE, c:c + MXU_TILE]))
    return jnp.concatenate(cols, axis=1)


def _mixout_kernel(x_ref, mod_ref, att_ref, gate_ref, xbr_ref, cw_ref, cb_ref, wa_ref, ba_ref,
                   wx_ref, bx_ref, lam_ref, wo_ref, o_ref, tail_ref, h_ref):
    d_att = att_ref.shape[2]

    @pl.when(pl.program_id(1) == 0)
    def _():
        tail_ref[...] = jnp.zeros_like(tail_ref)
        h_ref[...] = jnp.zeros_like(h_ref)

    neg = -lam_ref[...]
    softplus = jnp.maximum(neg, 0.0) + jnp.log1p(jnp.exp(-jnp.abs(neg)))
    decay_rate = (-RGLRU_C * LOG2E) * softplus

    sub_rows = [slice(r0, r0 + MIX_SUBTILE) for r0 in range(0, x_ref.shape[1], MIX_SUBTILE)]
    tail = tail_ref[...]
    decay_input = []
    for rows in sub_rows:
        xb = xbr_ref[0, rows, :]
        decay_input.append(_rglru_inputs(xb, tail, decay_rate, cw_ref, cb_ref, wa_ref, ba_ref, wx_ref, bx_ref))
        tail = xb[MIX_SUBTILE - SUBLANES:, :]
    tail_ref[...] = tail

    carry = h_ref[...]
    for rows, (a, u) in zip(sub_rows, decay_input):
        h, carry = _linear_scan(a, u, carry)
        rnn = (jax.nn.gelu(gate_ref[0, rows, :]) * h).astype(BF16)
        mix = _dot(att_ref[0, rows, :], wo_ref[0:d_att, :]) + _dot(rnn, wo_ref[d_att:, :])
        o_ref[0, rows, :] = x_ref[0, rows, :] + mod_ref[0, 5:6, :] * mix
    h_ref[...] = carry


def _rglru_inputs(xb, tail, decay_rate, cw_ref, cb_ref, wa_ref, ba_ref, wx_ref, bx_ref):
    tm, d_rnn = xb.shape
    n_groups = tm // SUBLANES
    ext = jnp.concatenate([tail, xb], axis=0)
    row_in_group = lax.broadcasted_iota(jnp.int32, (n_groups, SUBLANES, d_rnn), 1)
    row_2d = row_in_group.reshape(tm, d_rnn)
    ext_groups = ext.reshape(n_groups + 1, SUBLANES, d_rnn)
    xr = None
    for tap in range(CONV_WIDTH):
        back = CONV_WIDTH - 1 - tap
        if back == 0:
            shifted = xb
        else:
            rot = pltpu.roll(ext_groups, back, 1).reshape(ext.shape)
            shifted = jnp.where(row_2d >= back, rot[SUBLANES:, :], rot[:tm, :])
        term = shifted * cw_ref[tap:tap + 1, :]
        xr = term if xr is None else xr + term
    xr = xr + cb_ref[...]

    xr16 = xr.astype(BF16)
    r = jax.nn.sigmoid(_block_diag_dot(xr16, wa_ref) + ba_ref[...])
    ig = jax.nn.sigmoid(_block_diag_dot(xr16, wx_ref) + bx_ref[...])
    a = jnp.exp2(r * decay_rate)
    one_minus_a2 = 1.0 - a * a
    root = jnp.where(one_minus_a2 > 0.0, one_minus_a2 * lax.rsqrt(one_minus_a2), 0.0)
    return a, root * (ig * xr)


def _linear_scan(a, u, carry):
    tm, d_rnn = a.shape
    n_groups = tm // SUBLANES
    row_in_group = lax.broadcasted_iota(jnp.int32, (n_groups, SUBLANES, d_rnn), 1)
    a = a.reshape(n_groups, SUBLANES, d_rnn)
    u = u.reshape(n_groups, SUBLANES, d_rnn)
    step = 1
    while step < SUBLANES:
        valid = row_in_group >= step
        a_prev = jnp.where(valid, pltpu.roll(a, step, 1), 1.0)
        u_prev = jnp.where(valid, pltpu.roll(u, step, 1), 0.0)
        u = a * u_prev + u
        a = a * a_prev
        step *= 2
    groups = []
    for g in range(n_groups):
        h_g = u[g] + a[g] * carry
        carry = h_g[SUBLANES - 1:SUBLANES, :]
        groups.append(h_g)
    return jnp.concatenate(groups, axis=0), carry


def _mixout(x3, mod, att, gate3, xbr3, cw, cb, wa, ba, wx, bx, lam, wo):
    bsz, seq, d = x3.shape
    d_att = att.shape[2]
    d_rnn = xbr3.shape[2]
    tm = TOKEN_TILE

    def rows(width):
        return pl.BlockSpec((1, tm, width), lambda b, i: (b, i, 0))

    return pl.pallas_call(
        _mixout_kernel,
        out_shape=jax.ShapeDtypeStruct((bsz, seq, d), F32),
        grid=(bsz, seq // tm),
        in_specs=[
            rows(d),
            pl.BlockSpec((1,) + mod.shape[1:], lambda b, i: (b, 0, 0)),
            rows(d_att), rows(d_rnn), rows(d_rnn),
            _resident(cw.shape), _resident(cb.shape),
            _resident(wa.shape), _resident(ba.shape),
            _resident(wx.shape), _resident(bx.shape),
            _resident(lam.shape), _resident(wo.shape),
        ],
        out_specs=rows(d),
        scratch_shapes=[
            pltpu.VMEM((SUBLANES, d_rnn), F32),
            pltpu.VMEM((1, d_rnn), F32),
        ],
        compiler_params=_params("arbitrary", "arbitrary"),
        name="mixout",
    )(x3, mod, att, gate3, xbr3, cw, cb, wa, ba, wx, bx, lam, wo)


def _dense_block_diag(w):
    nb, r, _ = w.shape
    eye = jnp.eye(nb, dtype=w.dtype)
    return jnp.einsum("ncd,nm->ncmd", w, eye).reshape(nb * r, nb * r)


def kernel(x, c, w_ada, b_ada, g_norm, ffn1_w_gate, ffn1_w_up, ffn1_w_down, w_in, q_norm_g, k_norm_g,
           lambda_params, head_norm_g, conv_w, conv_b, w_rg_a, b_rg_a, w_rg_x, b_rg_x, lru_lambda, w_out,
           ffn2_w_gate, ffn2_w_up, ffn2_w_down):
    bsz, seq, d = x.shape
    depth = w_ada.shape[0]
    assert seq % TOKEN_TILE == 0 and TOKEN_TILE % ATT_TILE == 0 and ATT_TILE % CHUNK == 0
    assert w_in.shape[2] == 5 * N_HEADS * HEAD_W and q_norm_g.shape[1:] == (2, MAP_W)
    assert w_rg_a.shape[2] == RNN_BLOCK and MXU_TILE % RNN_BLOCK == 0 and conv_w.shape[1] == CONV_WIDTH
    assert depth == 1, "LAM_INIT is fixed for layer 0"
    assert CHUNK & (CHUNK - 1) == 0

    assert seq <= POS_SPLIT * 256, "position parts must stay exact in bf16"
    ones = _dense_block_diag(jnp.ones((MXU_TILE // MAP_W, MAP_W, MAP_W), BF16))
    pos = lax.broadcasted_iota(jnp.int32, (seq, HEAD_W), 0)
    lane = lax.broadcasted_iota(jnp.int32, (seq, HEAD_W), 1)
    kpos = jnp.where(lane < POS_PARTS, pos // POS_SPLIT,
                     jnp.where(lane < 2 * POS_PARTS, pos % POS_SPLIT, 0)).astype(BF16)
    for l in range(depth):
        mod = _adaln(c, w_ada[l], b_ada[l]).reshape(bsz, 9, d)
        gn = g_norm[l]

        x2d = _ffn(x.reshape(bsz * seq, d), mod, gn, ffn1_w_gate[l].astype(BF16),
                   ffn1_w_up[l].astype(BF16), ffn1_w_down[l].astype(BF16), sub=0, seq=seq)

        qg = jnp.tile(q_norm_g[l].reshape(1, HEAD_W), (1, N_HEADS))
        kg = jnp.tile(k_norm_g[l].reshape(1, HEAD_W), (1, N_HEADS))
        qt, k2d, vt, gate2d, xbr2d = _inproj(x2d, mod, gn, w_in[l].astype(BF16), ones, qg, kg,
                                             bsz=bsz, seq=seq)
        d_att = k2d.shape[1]
        att = _attention(lambda_params[l], head_norm_g[l].reshape(1, HEAD_W), kpos, qt,
                         k2d.reshape(bsz, seq, d_att), vt)
        x3 = _mixout(x2d.reshape(bsz, seq, d), mod, att,
                     gate2d.reshape(bsz, seq, -1), xbr2d.reshape(bsz, seq, -1),
                     conv_w[l], conv_b[l].reshape(1, -1),
                     _dense_block_diag(w_rg_a[l]).astype(BF16), b_rg_a[l].reshape(1, -1),
                     _dense_block_diag(w_rg_x[l]).astype(BF16), b_rg_x[l].reshape(1, -1),
                     lru_lambda[l].reshape(1, -1), w_out[l].astype(BF16))

        x = _ffn(x3.reshape(bsz * seq, d), mod, gn, ffn2_w_gate[l].astype(BF16),
                 ffn2_w_up[l].astype(BF16), ffn2_w_down[l].astype(BF16), sub=2, seq=seq).reshape(bsz, seq, d)
    return x
```

```python
import functools
import math

import jax
import jax.numpy as jnp
import numpy as np
from jax import lax
from jax.experimental import pallas as pl
from jax.experimental.pallas import tpu as pltpu

F32 = jnp.float32
BF16 = jnp.bfloat16

EPS = 1e-6
CHUNK = 64
N_HEADS = 4
HEAD_W = 128
MAP_W = 64
RNN_BLOCK = 64
CONV_WIDTH = 4
RGLRU_C = 8.0
LAM_INIT = 0.8 - 0.6 * math.exp(-0.3 * 0)
ALIBI_SLOPES = tuple(2.0 ** (-8.0 * (h + 1) / N_HEADS) for h in range(N_HEADS))
MASK_VALUE = -1e30
LOG2E = math.log2(math.e)
POS_SPLIT = 64
POS_PARTS = 3
VT_ROWS = HEAD_W + 16

TOKEN_TILE = 512
MIX_SUBTILE = 128
ATT_TILE = 256
MXU_TILE = 256
SUBLANES = 8
ADALN_COLS = 1024
VMEM_LIMIT_BYTES = 56 * 1024 * 1024


def _params(*semantics):
    return pltpu.CompilerParams(dimension_semantics=semantics, vmem_limit_bytes=VMEM_LIMIT_BYTES)


def _resident(shape):
    zeros = (0,) * len(shape)
    return pl.BlockSpec(shape, lambda *_: zeros, pipeline_mode=pl.Buffered(1))


def _split_bf16(x):
    hi = x.astype(BF16)
    lo = (x - hi.astype(F32)).astype(BF16)
    return hi, lo


def _bf16_parts(value):
    parts, rest = [], float(value)
    for _ in range(POS_PARTS):
        part = float(np.asarray(rest, dtype=BF16))
        parts.append(part)
        rest -= part
    return parts


def _dot(a, b):
    return jnp.dot(a, b, preferred_element_type=F32)


def _norm_mod(x, g, shift, scale):
    y = x * lax.rsqrt(jnp.mean(x * x, axis=-1, keepdims=True) + EPS)
    return (y * g) * (1.0 + scale) + shift


def _adaln_kernel(c_ref, w_ref, b_ref, o_ref):
    c = c_ref[...]
    c_act = c * jax.nn.sigmoid(c)
    c_hi, c_lo = _split_bf16(c_act)
    w_hi, w_lo = _split_bf16(w_ref[...])
    o_ref[...] = (_dot(c_hi, w_hi) + (_dot(c_hi, w_lo) + _dot(c_lo, w_hi))) + b_ref[...]


def _adaln(c, w, b):
    bsz, d = c.shape
    n = w.shape[1]
    return pl.pallas_call(
        _adaln_kernel,
        out_shape=jax.ShapeDtypeStruct((bsz, n), F32),
        grid=(n // ADALN_COLS,),
        in_specs=[
            pl.BlockSpec((bsz, d), lambda j: (0, 0)),
            pl.BlockSpec((d, ADALN_COLS), lambda j: (0, j)),
            pl.BlockSpec((1, ADALN_COLS), lambda j: (0, j)),
        ],
        out_specs=pl.BlockSpec((bsz, ADALN_COLS), lambda j: (0, j)),
        compiler_params=_params("parallel"),
        name="adaln",
    )(c, w, b.reshape(1, n))


def _normed_tile(x_ref, mod_ref, gn_ref, sub):
    row = 3 * sub
    return _norm_mod(x_ref[...], gn_ref[sub:sub + 1, :], mod_ref[0, row:row + 1, :],
                     mod_ref[0, row + 1:row + 2, :]).astype(BF16)


def _swiglu_residual(x, h, gate, wg_ref, wu_ref, wd_ref):
    g = _dot(h, wg_ref[...])
    u = _dot(h, wu_ref[...])
    a = ((g * jax.nn.sigmoid(g)) * u).astype(BF16)
    return x + (0.5 * gate) * _dot(a, wd_ref[...])


def _ffn_kernel(x_ref, mod_ref, gn_ref, wg_ref, wu_ref, wd_ref, o_ref, *, sub):
    h = _normed_tile(x_ref, mod_ref, gn_ref, sub)
    o_ref[...] = _swiglu_residual(x_ref[...], h, mod_ref[0, 3 * sub + 2:3 * sub + 3, :], wg_ref, wu_ref, wd_ref)


def _outproj_ffn_kernel(x_ref, mod_ref, att_ref, rnn_ref, wo_ref, gn_ref, wg_ref, wu_ref, wd_ref, o_ref):
    d_att = att_ref.shape[1]
    mix = _dot(att_ref[...], wo_ref[0:d_att, :]) + _dot(rnn_ref[...], wo_ref[d_att:, :])
    x = x_ref[...] + mod_ref[0, 5:6, :] * mix
    h = _norm_mod(x, gn_ref[2:3, :], mod_ref[0, 6:7, :], mod_ref[0, 7:8, :]).astype(BF16)
    o_ref[...] = _swiglu_residual(x, h, mod_ref[0, 8:9, :], wg_ref, wu_ref, wd_ref)


def _outproj_ffn(x2d, mod, att2d, rnn2d, wo, g_norm, wg, wu, wd, *, seq):
    n, d = x2d.shape
    f = wg.shape[1]
    tiles_per_seq = seq // TOKEN_TILE

    def rows(width):
        return pl.BlockSpec((TOKEN_TILE, width), lambda i: (i, 0))

    return pl.pallas_call(
        _outproj_ffn_kernel,
        out_shape=jax.ShapeDtypeStruct((n, d), F32),
        grid=(n // TOKEN_TILE,),
        in_specs=[
            rows(d),
            pl.BlockSpec((1,) + mod.shape[1:], lambda i: (i // tiles_per_seq, 0, 0)),
            rows(att2d.shape[1]),
            rows(rnn2d.shape[1]),
            _resident(wo.shape),
            _resident(g_norm.shape),
            _resident((d, f)),
            _resident((d, f)),
            _resident((f, d)),
        ],
        out_specs=rows(d),
        compiler_params=_params("parallel"),
        name="outproj_ffn2",
    )(x2d, mod, att2d, rnn2d, wo, g_norm, wg, wu, wd)


def _ffn(x2d, mod, g_norm, wg, wu, wd, *, sub, seq):
    n, d = x2d.shape
    f = wg.shape[1]
    tiles_per_seq = seq // TOKEN_TILE
    return pl.pallas_call(
        functools.partial(_ffn_kernel, sub=sub),
        out_shape=jax.ShapeDtypeStruct((n, d), F32),
        grid=(n // TOKEN_TILE,),
        in_specs=[
            pl.BlockSpec((TOKEN_TILE, d), lambda i: (i, 0)),
            pl.BlockSpec((1,) + mod.shape[1:], lambda i: (i // tiles_per_seq, 0, 0)),
            _resident(g_norm.shape),
            _resident((d, f)),
            _resident((d, f)),
            _resident((f, d)),
        ],
        out_specs=pl.BlockSpec((TOKEN_TILE, d), lambda i: (i, 0)),
        compiler_params=_params("parallel"),
        name=f"ffn{sub}",
    )(x2d, mod, g_norm, wg, wu, wd)


def _group_mean_sq(x, ones_ref):
    sq = (x * x).astype(BF16)
    ones = ones_ref[...]
    cols = [_dot(sq[:, c:c + MXU_TILE], ones) for c in range(0, x.shape[1], MXU_TILE)]
    return jnp.concatenate(cols, axis=1) * (1.0 / MAP_W)


def _inproj_kernel(x_ref, mod_ref, gn_ref, w_qkv_ref, w_rnn_ref, ones_ref, qg_ref, kg_ref,
                   cw_ref, cb_ref, wa_ref, ba_ref, wx_ref, bx_ref, lam_ref,
                   qt_ref, k_ref, vt_ref, rnn_ref, tail_ref, state_ref, proj_ref):
    @pl.when(pl.program_id(1) == 0)
    def _():
        tail_ref[...] = jnp.zeros_like(tail_ref)
        state_ref[...] = jnp.zeros_like(state_ref)

    h = _normed_tile(x_ref, mod_ref, gn_ref, 1)
    d_rnn = rnn_ref.shape[1]
    tm = h.shape[0]

    rnn_proj = _dot(h, w_rnn_ref[...])
    neg = -lam_ref[...]
    softplus = jnp.maximum(neg, 0.0) + jnp.log1p(jnp.exp(-jnp.abs(neg)))
    decay_rate = (-RGLRU_C * LOG2E) * softplus
    sub_rows = [slice(r0, r0 + MIX_SUBTILE) for r0 in range(0, tm, MIX_SUBTILE)]
    qkv_cols = proj_ref.shape[1]
    col_step = -(-qkv_cols // ((len(sub_rows) - 1) * MXU_TILE)) * MXU_TILE
    tail = tail_ref[...]
    decay_input = []
    for s, rows in enumerate(sub_rows):
        xb = rnn_proj[rows, d_rnn:]
        decay_input.append(_rglru_inputs(xb, tail, decay_rate, cw_ref, cb_ref, wa_ref, ba_ref, wx_ref, bx_ref))
        tail = xb[MIX_SUBTILE - SUBLANES:, :]
        cols = slice(min(s * col_step, qkv_cols), min((s + 1) * col_step, qkv_cols))
        if cols.start < cols.stop:
            proj_ref[:, cols] = _dot(h, w_qkv_ref[:, cols])
    tail_ref[...] = tail

    carry = state_ref[...]
    for rows, (a, u) in zip(sub_rows, decay_input):
        hr, carry = _linear_scan(a, u, carry)
        rnn_ref[rows, :] = (jax.nn.gelu(rnn_proj[rows, 0:d_rnn]) * hr).astype(BF16)
    state_ref[...] = carry

    d_att = N_HEADS * HEAD_W
    q = proj_ref[:, 0:d_att]
    k = proj_ref[:, d_att:2 * d_att]
    v = proj_ref[:, 2 * d_att:3 * d_att]

    qn = (q * lax.rsqrt(_group_mean_sq(q, ones_ref) + EPS)) * qg_ref[...] * (MAP_W ** -0.5 * LOG2E)
    kn = (k * lax.rsqrt(_group_mean_sq(k, ones_ref) + EPS)) * kg_ref[...]
    k_ref[...] = kn.astype(BF16)
    for hd in range(N_HEADS):
        lanes = slice(hd * HEAD_W, (hd + 1) * HEAD_W)
        qt_ref[0, hd] = qn[:, lanes].T.astype(BF16)
        for c in range(tm // ATT_TILE):
            rows = slice(c * ATT_TILE, (c + 1) * ATT_TILE)
            vt_ref[0, hd, c, 0:HEAD_W, :] = v[rows, lanes].T.astype(BF16)
            vt_ref[0, hd, c, HEAD_W:, :] = jnp.ones((VT_ROWS - HEAD_W, ATT_TILE), BF16)


def _inproj(x2d, mod, g_norm, w_qkv, w_rnn, ones, qg, kg, rglru_params, *, bsz, seq):
    n, d = x2d.shape
    d_att = N_HEADS * HEAD_W
    d_rnn = w_rnn.shape[1] // 2
    tiles_per_seq = seq // TOKEN_TILE
    blocks_per_tile = TOKEN_TILE // ATT_TILE

    def rows(width):
        return pl.BlockSpec((TOKEN_TILE, width), lambda b, i: (b * tiles_per_seq + i, 0))

    return pl.pallas_call(
        _inproj_kernel,
        out_shape=(
            jax.ShapeDtypeStruct((bsz, N_HEADS, HEAD_W, seq), BF16),
            jax.ShapeDtypeStruct((n, d_att), BF16),
            jax.ShapeDtypeStruct((bsz, N_HEADS, seq // ATT_TILE, VT_ROWS, ATT_TILE), BF16),
            jax.ShapeDtypeStruct((n, d_rnn), BF16),
        ),
        grid=(bsz, tiles_per_seq),
        in_specs=[
            rows(d),
            pl.BlockSpec((1,) + mod.shape[1:], lambda b, i: (b, 0, 0)),
            _resident(g_norm.shape),
            _resident(w_qkv.shape),
            _resident(w_rnn.shape),
            _resident(ones.shape),
            _resident(qg.shape),
            _resident(kg.shape),
            *[_resident(p.shape) for p in rglru_params],
        ],
        out_specs=(
            pl.BlockSpec((1, N_HEADS, HEAD_W, TOKEN_TILE), lambda b, i: (b, 0, 0, i)),
            rows(d_att),
            pl.BlockSpec((1, N_HEADS, blocks_per_tile, VT_ROWS, ATT_TILE), lambda b, i: (b, 0, i, 0, 0)),
            rows(d_rnn),
        ),
        scratch_shapes=[
            pltpu.VMEM((SUBLANES, d_rnn), F32),
            pltpu.VMEM((1, d_rnn), F32),
            pltpu.VMEM((TOKEN_TILE, 3 * d_att), F32),
        ],
        compiler_params=_params("arbitrary", "arbitrary"),
        name="inproj_rglru",
    )(x2d, mod, g_norm, w_qkv, w_rnn, ones, qg, kg, *rglru_params)


def _attn_kernel(lp_ref, hg_ref, kpos_ref, qfeat_ref, dbias_ref, qt_ref, k_ref, vt_ref, o_ref,
                 m_ref, acc_ref, qq_ref, s_ref):
    t = ATT_TILE
    i = pl.program_id(1)
    lp = lp_ref[...]
    lam = (jnp.exp(jnp.sum(lp[0:1] * lp[1:2], axis=-1, keepdims=True))
           - jnp.exp(jnp.sum(lp[2:3] * lp[3:4], axis=-1, keepdims=True)) + LAM_INIT)

    def scores(j, hd, qq):
        rows = pl.ds(pl.multiple_of(j * t, t), t)
        kj = jnp.concatenate([k_ref[0, rows, hd * HEAD_W:(hd + 1) * HEAD_W], kpos_ref[rows, :]], axis=1)
        return _dot(kj, jnp.concatenate([qq, qfeat_ref[hd]], axis=0))

    map_row = lax.broadcasted_iota(jnp.int32, (HEAD_W, t), 0)
    for hd in range(N_HEADS):
        qt = qt_ref[0, hd]
        zero = jnp.zeros_like(qt)
        qq = jnp.concatenate(
            [jnp.where(map_row < MAP_W, qt, zero), jnp.where(map_row >= MAP_W, qt, zero)], axis=1)
        qq_ref[hd] = qq
        s_ref[0, hd] = scores(0, hd, qq)
        m_ref[hd] = jnp.full((1, 2 * t), MASK_VALUE, F32)
        acc_ref[hd] = jnp.zeros(acc_ref.shape[1:], F32)

    def update(hd, s, vt):
        m_old = m_ref[hd]
        m_new = jnp.maximum(m_old, jnp.max(s, axis=0, keepdims=True))
        alpha = jnp.exp2(m_old - m_new)
        p = jnp.exp2(s - m_new).astype(BF16)
        acc_ref[hd] = alpha * acc_ref[hd] + _dot(vt, p)
        m_ref[hd] = m_new

    def step(j, slot):
        for hd in range(N_HEADS):
            s_ref[1 - slot, hd] = scores(j + 1, hd, qq_ref[hd])
            update(hd, s_ref[slot, hd], vt_ref[0, hd, j])

    def visible_pair(jj, carry):
        step(2 * jj, 0)
        step(2 * jj + 1, 1)
        return carry

    odd = i & 1
    lax.fori_loop(0, lax.shift_right_logical(i, 1), visible_pair, 0)

    @pl.when(odd == 1)
    def _():
        step(i - 1, 0)

    for hd in range(N_HEADS):
        update(hd, s_ref[odd, hd] + dbias_ref[hd], vt_ref[0, hd, i])

    for hd in range(N_HEADS):
        acc = acc_ref[hd]
        o = acc[0:HEAD_W] * (1.0 / acc[HEAD_W:HEAD_W + 1])
        diff = o[:, :t] - lam * o[:, t:]
        y = diff * lax.rsqrt(jnp.mean(diff * diff, axis=0, keepdims=True) + EPS)
        y = (y.T * hg_ref[...]) * (1.0 - LAM_INIT)
        o_ref[0, :, hd * HEAD_W:(hd + 1) * HEAD_W] = y.astype(BF16)


def _alibi_tables(seq):
    t = ATT_TILE
    pos = lax.broadcasted_iota(jnp.int32, (seq, HEAD_W), 0)
    lane = lax.broadcasted_iota(jnp.int32, (seq, HEAD_W), 1)
    kpos = jnp.where(lane < POS_PARTS, pos // POS_SPLIT,
                     jnp.where(lane < 2 * POS_PARTS, pos % POS_SPLIT, 0)).astype(BF16)

    feat_row = lax.broadcasted_iota(jnp.int32, (HEAD_W, 2 * t), 0)
    key = lax.broadcasted_iota(jnp.int32, (t, 2 * t), 0)
    qry = lax.broadcasted_iota(jnp.int32, (t, 2 * t), 1) % t
    visible = (key // CHUNK) <= (qry // CHUNK)
    ahead = jnp.maximum(key - qry, 0).astype(F32)
    qfeat, dbias = [], []
    for slope in ALIBI_SLOPES:
        feat = jnp.zeros((HEAD_W, 2 * t), F32)
        for r, part in enumerate(_bf16_parts(slope * LOG2E)):
            feat = jnp.where(feat_row == r, part * POS_SPLIT, feat)
            feat = jnp.where(feat_row == POS_PARTS + r, part, feat)
        qfeat.append(feat.astype(BF16))
        dbias.append(jnp.where(visible, (-2.0 * LOG2E * slope) * ahead, MASK_VALUE))
    return kpos, jnp.stack(qfeat), jnp.stack(dbias)


def _attention(lp, hg, qt, k3, vt):
    bsz, _, _, seq = qt.shape
    t = ATT_TILE
    d_att = N_HEADS * HEAD_W
    kpos, qfeat, dbias = _alibi_tables(seq)
    return pl.pallas_call(
        _attn_kernel,
        out_shape=jax.ShapeDtypeStruct((bsz, seq, d_att), BF16),
        grid=(bsz, seq // t),
        in_specs=[
            _resident(lp.shape),
            _resident(hg.shape),
            _resident(kpos.shape),
            _resident(qfeat.shape),
            _resident(dbias.shape),
            pl.BlockSpec((1, N_HEADS, HEAD_W, t), lambda b, i: (b, 0, 0, i)),
            pl.BlockSpec((1, seq, d_att), lambda b, i: (b, 0, 0)),
            pl.BlockSpec((1,) + vt.shape[1:], lambda b, i: (b, 0, 0, 0, 0)),
        ],
        out_specs=pl.BlockSpec((1, t, d_att), lambda b, i: (b, i, 0)),
        scratch_shapes=[
            pltpu.VMEM((N_HEADS, 1, 2 * t), F32),
            pltpu.VMEM((N_HEADS, VT_ROWS, 2 * t), F32),
            pltpu.VMEM((N_HEADS, HEAD_W, 2 * t), BF16),
            pltpu.VMEM((2, N_HEADS, t, 2 * t), F32),
        ],
        compiler_params=_params("parallel", "arbitrary"),
        name="diff_attention",
    )(lp, hg, kpos, qfeat, dbias, qt, k3, vt)


def _block_diag_dot(x, w_ref):
    cols = []
    for c in range(0, x.shape[1], MXU_TILE):
        cols.append(_dot(x[:, c:c + MXU_TILE], w_ref[c:c + MXU_TILE, c:c + MXU_TILE]))
    return jnp.concatenate(cols, axis=1)


def _rglru_inputs(xb, tail, decay_rate, cw_ref, cb_ref, wa_ref, ba_ref, wx_ref, bx_ref):
    tm, d_rnn = xb.shape
    n_groups = tm // SUBLANES
    ext = jnp.concatenate([tail, xb], axis=0)
    row_in_group = lax.broadcasted_iota(jnp.int32, (n_groups, SUBLANES, d_rnn), 1)
    row_2d = row_in_group.reshape(tm, d_rnn)
    ext_groups = ext.reshape(n_groups + 1, SUBLANES, d_rnn)
    xr = None
    for tap in range(CONV_WIDTH):
        back = CONV_WIDTH - 1 - tap
        if back == 0:
            shifted = xb
        else:
            rot = pltpu.roll(ext_groups, back, 1).reshape(ext.shape)
            shifted = jnp.where(row_2d >= back, rot[SUBLANES:, :], rot[:tm, :])
        term = shifted * cw_ref[tap:tap + 1, :]
        xr = term if xr is None else xr + term
    xr = xr + cb_ref[...]

    xr16 = xr.astype(BF16)
    r = jax.nn.sigmoid(_block_diag_dot(xr16, wa_ref) + ba_ref[...])
    ig = jax.nn.sigmoid(_block_diag_dot(xr16, wx_ref) + bx_ref[...])
    a = jnp.exp2(r * decay_rate)
    one_minus_a2 = 1.0 - a * a
    root = jnp.where(one_minus_a2 > 0.0, one_minus_a2 * lax.rsqrt(one_minus_a2), 0.0)
    return a, root * (ig * xr)


def _linear_scan(a, u, carry):
    tm, d_rnn = a.shape
    n_groups = tm // SUBLANES
    row_in_group = lax.broadcasted_iota(jnp.int32, (n_groups, SUBLANES, d_rnn), 1)
    a = a.reshape(n_groups, SUBLANES, d_rnn)
    u = u.reshape(n_groups, SUBLANES, d_rnn)
    step = 1
    while step < SUBLANES:
        valid = row_in_group >= step
        a_prev = jnp.where(valid, pltpu.roll(a, step, 1), 1.0)
        u_prev = jnp.where(valid, pltpu.roll(u, step, 1), 0.0)
        u = a * u_prev + u
        a = a * a_prev
        step *= 2
    groups = []
    for g in range(n_groups):
        h_g = u[g] + a[g] * carry
        carry = h_g[SUBLANES - 1:SUBLANES, :]
        groups.append(h_g)
    return jnp.concatenate(groups, axis=0), carry


def _dense_block_diag(w):
    nb, r, _ = w.shape
    eye = jnp.eye(nb, dtype=w.dtype)
    return jnp.einsum("ncd,nm->ncmd", w, eye).reshape(nb * r, nb * r)


def kernel(x, c, w_ada, b_ada, g_norm, ffn1_w_gate, ffn1_w_up, ffn1_w_down, w_in, q_norm_g, k_norm_g,
           lambda_params, head_norm_g, conv_w, conv_b, w_rg_a, b_rg_a, w_rg_x, b_rg_x, lru_lambda, w_out,
           ffn2_w_gate, ffn2_w_up, ffn2_w_down):
    bsz, seq, d = x.shape
    depth = w_ada.shape[0]
    assert seq % TOKEN_TILE == 0 and TOKEN_TILE % ATT_TILE == 0 and ATT_TILE % CHUNK == 0
    assert w_in.shape[2] == 5 * N_HEADS * HEAD_W and q_norm_g.shape[1:] == (2, MAP_W)
    assert w_rg_a.shape[2] == RNN_BLOCK and MXU_TILE % RNN_BLOCK == 0 and conv_w.shape[1] == CONV_WIDTH
    assert depth == 1, "LAM_INIT is fixed for layer 0"
    assert CHUNK & (CHUNK - 1) == 0

    assert seq <= POS_SPLIT * 256, "position parts must stay exact in bf16"
    ones = _dense_block_diag(jnp.ones((MXU_TILE // MAP_W, MAP_W, MAP_W), BF16))
    for l in range(depth):
        mod = _adaln(c, w_ada[l], b_ada[l]).reshape(bsz, 9, d)
        gn = g_norm[l]

        x2d = _ffn(x.reshape(bsz * seq, d), mod, gn, ffn1_w_gate[l].astype(BF16),
                   ffn1_w_up[l].astype(BF16), ffn1_w_down[l].astype(BF16), sub=0, seq=seq)

        qg = jnp.tile(q_norm_g[l].reshape(1, HEAD_W), (1, N_HEADS))
        kg = jnp.tile(k_norm_g[l].reshape(1, HEAD_W), (1, N_HEADS))
        d_att = N_HEADS * HEAD_W
        w_in16 = w_in[l].astype(BF16)
        rglru_params = (conv_w[l], conv_b[l].reshape(1, -1),
                        _dense_block_diag(w_rg_a[l]).astype(BF16), b_rg_a[l].reshape(1, -1),
                        _dense_block_diag(w_rg_x[l]).astype(BF16), b_rg_x[l].reshape(1, -1),
                        lru_lambda[l].reshape(1, -1))
        qt, k2d, vt, rnn2d = _inproj(x2d, mod, gn, w_in16[:, :3 * d_att], w_in16[:, 3 * d_att:], ones, qg, kg,
                                     rglru_params, bsz=bsz, seq=seq)
        att = _attention(lambda_params[l], head_norm_g[l].reshape(1, HEAD_W), qt,
                         k2d.reshape(bsz, seq, d_att), vt)
        x = _outproj_ffn(x2d, mod, att.reshape(bsz * seq, d_att), rnn2d, w_out[l].astype(BF16), gn,
                         ffn2_w_gate[l].astype(BF16), ffn2_w_up[l].astype(BF16), ffn2_w_down[l].astype(BF16),
                         seq=seq).reshape(bsz, seq, d)
    return x
```
